```python
import jax, jax.numpy as jnp
from jax import lax
import numpy as np

D_MODEL = 1024
BATCH = 8
SEQ = 8192
DEPTH = 1

CHUNK = 64
HEAD_SIZE = 64
D_RWKV = D_MODEL
N_RWKV_HEADS = D_RWKV // HEAD_SIZE
D_CONV = D_MODEL
CONV_WIDTH = 31
D_DECAY_LORA = 64
D_AAA_LORA = 64
D_GATE_LORA = 128
D_FF = 4 * D_MODEL
N_BRANCHES = 2
RMS_EPS = 1e-6
LN_EPS = 1e-5
GN_EPS = 64e-5
DECAY_OFFSET = 0.5
COL_RKV = 3 * D_RWKV
COL_GLU = 2 * D_CONV
COL_GATE = N_BRANCHES * D_MODEL
IN_COLS = COL_RKV + COL_GLU + COL_GATE

kernel_name = "hybrid_rwkv7_conformer_gated_block"


def _rmsnorm(x, g):
    xf = x.astype(jnp.float32)
    y = xf * lax.rsqrt(jnp.mean(xf * xf, axis=-1, keepdims=True) + RMS_EPS)
    return (y * g.astype(jnp.float32)).astype(x.dtype)


def _layernorm(x, g, b):
    xf = x.astype(jnp.float32)
    mu = jnp.mean(xf, axis=-1, keepdims=True)
    var = jnp.mean(jnp.square(xf - mu), axis=-1, keepdims=True)
    y = (xf - mu) * lax.rsqrt(var + LN_EPS)
    return (y * g.astype(jnp.float32) + b.astype(jnp.float32)).astype(x.dtype)


def _token_shift(z):
    return jnp.pad(z, ((0, 0), (1, 0), (0, 0)))[:, :-1]


def _heads(z):
    b, t, _ = z.shape
    return z.reshape(b, t, N_RWKV_HEADS, HEAD_SIZE)


def _wkv7_scan(r, w, k, v, kk, a):
    b, t, h, n = r.shape
    n_chunks = t // CHUNK

    def to_chunks(z):
        return jnp.moveaxis(z.astype(jnp.float32).reshape(b, n_chunks, CHUNK, h, n), (1, 2), (0, 1))

    def frame_step(S, inp):
        r_t, w_t, k_t, v_t, kk_t, a_t = inp
        sa = jnp.einsum('bhvk,bhk->bhv', S, -kk_t)
        S = (S * w_t[:, :, None, :]
             + sa[..., None] * (kk_t * a_t)[:, :, None, :]
             + v_t[..., None] * k_t[:, :, None, :])
        o_t = jnp.einsum('bhvk,bhk->bhv', S, r_t)
        return S, o_t

    def chunk_step(S, chunk_inp):
        return lax.scan(frame_step, S, chunk_inp)

    S0 = jnp.zeros((b, h, n, n), jnp.float32)
    inputs = (to_chunks(r), to_chunks(w), to_chunks(k), to_chunks(v), to_chunks(kk), to_chunks(a))
    _, o = lax.scan(chunk_step, S0, inputs)
    return jnp.moveaxis(o, (0, 1), (1, 2)).reshape(b, t, h, n)


def _rwkv7_mixer(h, r, k, v, mu_lora, w0, w1, w2, a0, a1, a2, g1, g2, k_k, k_a, r_k, ln_g, ln_b):
    dh = _token_shift(h) - h
    xw = h + dh * mu_lora[0]
    xa = h + dh * mu_lora[1]
    xg = h + dh * mu_lora[2]
    w_log = -jax.nn.softplus(-(w0 + jnp.tanh(xw @ w1) @ w2)) - DECAY_OFFSET
    decay = jnp.exp(-jnp.exp(w_log.astype(jnp.float32)))
    a = jax.nn.sigmoid(a0 + (xa @ a1) @ a2)
    g = jax.nn.sigmoid(xg @ g1) @ g2
    kk = _heads(k * k_k).astype(jnp.float32)
    kk = kk / jnp.maximum(jnp.linalg.norm(kk, axis=-1, keepdims=True), 1e-12)
    k = k * (1.0 + (a - 1.0) * k_a)
    rh, kh, vh, ah = _heads(r), _heads(k), _heads(v), _heads(a)
    o = _wkv7_scan(rh, _heads(decay), kh, vh, kk, ah)
    mu = jnp.mean(o, axis=-1, keepdims=True)
    var = jnp.mean(jnp.square(o - mu), axis=-1, keepdims=True)
    o = ((o - mu) * lax.rsqrt(var + GN_EPS)).reshape(h.shape[0], h.shape[1], D_RWKV)
    o = (o * ln_g.astype(jnp.float32) + ln_b.astype(jnp.float32)).astype(h.dtype)
    bonus = jnp.sum(rh * kh * r_k, axis=-1, keepdims=True) * vh
    o = o + bonus.reshape(o.shape)
    return o * g


def _conformer_conv_mixer(glu_in, conv_w, conv_b, ln_g, ln_b):
    u = glu_in[..., :D_CONV] * jax.nn.sigmoid(glu_in[..., D_CONV:])
    u = lax.conv_general_dilated(
        u, conv_w[:, None, :], window_strides=(1,), padding=((CONV_WIDTH - 1, 0),),
        dimension_numbers=('NWC', 'WIO', 'NWC'), feature_group_count=D_CONV) + conv_b
    u = _layernorm(u, ln_g, ln_b)
    return jax.nn.silu(u)


def setup_inputs(seed: int = 0) -> dict:
    key = jax.random.key(seed)
    ks = iter(jax.random.split(key, 40))
    L = DEPTH
    f32 = jnp.float32

    def nrm(shape, scale):
        return jax.random.normal(next(ks), shape, f32) * scale

    def unif(shape, lo, hi):
        return jax.random.uniform(next(ks), shape, f32, minval=lo, maxval=hi)

    return {
        "x": nrm((BATCH, SEQ, D_MODEL), 1.0),
        "norm_mix_g": 1.0 + nrm((L, D_MODEL), 0.05),
        "w_in": nrm((L, D_MODEL, IN_COLS), D_MODEL ** -0.5),
        "b_gate": nrm((L, COL_GATE), 0.1),
        "mu_rkv": unif((L, COL_RKV), 0.0, 1.0),
        "mu_lora": unif((L, 3, D_MODEL), 0.0, 1.0),
        "decay_w0": unif((L, D_RWKV), -3.0, 2.0),
        "decay_w1": nrm((L, D_MODEL, D_DECAY_LORA), D_MODEL ** -0.5),
        "decay_w2": nrm((L, D_DECAY_LORA, D_RWKV), 0.1),
        "aaa_a0": nrm((L, D_RWKV), 0.1),
        "aaa_a1": nrm((L, D_MODEL, D_AAA_LORA), D_MODEL ** -0.5),
        "aaa_a2": nrm((L, D_AAA_LORA, D_RWKV), 0.5 * D_AAA_LORA ** -0.5),
        "gate_g1": nrm((L, D_MODEL, D_GATE_LORA), D_MODEL ** -0.5),
        "gate_g2": nrm((L, D_GATE_LORA, D_RWKV), D_GATE_LORA ** -0.5),
        "k_k": 0.85 + nrm((L, D_RWKV), 0.05),
        "k_a": 1.0 + nrm((L, D_RWKV), 0.05),
        "r_k": nrm((L, N_RWKV_HEADS, HEAD_SIZE), 0.1),
        "ln_x_g": 1.0 + nrm((L, D_RWKV), 0.05),
        "ln_x_b": nrm((L, D_RWKV), 0.02),
        "w_rwkv_proj": nrm((L, D_RWKV, D_MODEL), D_RWKV ** -0.5),
        "conv_w": nrm((L, CONV_WIDTH, D_CONV), CONV_WIDTH ** -0.5),
        "conv_b": nrm((L, D_CONV), 0.02),
        "conv_ln_g": 1.0 + nrm((L, D_CONV), 0.05),
        "conv_ln_b": nrm((L, D_CONV), 0.02),
        "w_conv_proj": nrm((L, D_CONV, D_MODEL), D_CONV ** -0.5),
        "w_out": nrm((L, D_MODEL, D_MODEL), D_MODEL ** -0.5),
        "norm_ff_g": 1.0 + nrm((L, D_MODEL), 0.05),
        "w_ff1": nrm((L, D_MODEL, D_FF), D_MODEL ** -0.5),
        "w_ff2": nrm((L, D_FF, D_MODEL), 0.5 * D_FF ** -0.5),
        "norm_final_g": 1.0 + nrm((D_MODEL,), 0.05),
    }


def reference(x, norm_mix_g, w_in, b_gate, mu_rkv, mu_lora, decay_w0, decay_w1, decay_w2,
              aaa_a0, aaa_a1, aaa_a2, gate_g1, gate_g2, k_k, k_a, r_k, ln_x_g, ln_x_b,
              w_rwkv_proj, conv_w, conv_b, conv_ln_g, conv_ln_b, w_conv_proj, w_out,
              norm_ff_g, w_ff1, w_ff2, norm_final_g):
    b, t, _ = x.shape
    for l in range(DEPTH):
        h = _rmsnorm(x, norm_mix_g[l])
        proj = h @ w_in[l]
        rkv = proj[..., :COL_RKV]
        glu_in = proj[..., COL_RKV:COL_RKV + COL_GLU]
        gate_logits = proj[..., COL_RKV + COL_GLU:] + b_gate[l]
        rkv = rkv + (_token_shift(rkv) - rkv) * mu_rkv[l]
        r = rkv[..., :D_RWKV]
        k = rkv[..., D_RWKV:2 * D_RWKV]
        v = rkv[..., 2 * D_RWKV:]
        a_out = _rwkv7_mixer(h, r, k, v, mu_lora[l], decay_w0[l], decay_w1[l], decay_w2[l],
                             aaa_a0[l], aaa_a1[l], aaa_a2[l], gate_g1[l], gate_g2[l],
                             k_k[l], k_a[l], r_k[l], ln_x_g[l], ln_x_b[l])
        c_out = _conformer_conv_mixer(glu_in, conv_w[l], conv_b[l], conv_ln_g[l], conv_ln_b[l])
        gates = jax.nn.sigmoid(gate_logits).reshape(b, t, N_BRANCHES, D_MODEL)
        merged = gates[:, :, 0] * (a_out @ w_rwkv_proj[l]) + gates[:, :, 1] * (c_out @ w_conv_proj[l])
        x = x + merged @ w_out[l]
        h2 = _rmsnorm(x, norm_ff_g[l])
        x = x + jnp.square(jax.nn.relu(h2 @ w_ff1[l])) @ w_ff2[l]
    return _rmsnorm(x, norm_final_g)
```

```python
import functools
import math

import jax
import jax.numpy as jnp
from jax import lax
from jax.experimental import pallas as pl
from jax.experimental.pallas import tpu as pltpu

HEAD_SIZE = 64
WKV_CHUNK = 64
HEADS_PER_GROUP = 4
CONV_WIDTH = 31
CONV_HALO = 32
RMS_EPS = 1e-6
LN_EPS = 1e-5
GN_EPS = 64e-5
DECAY_SCALE = math.exp(-0.5)
SUBLANES = 8

VMEM_LIMIT_BYTES = 56 * 1024 * 1024

F32 = jnp.float32
BF16 = jnp.bfloat16


def _dot(a, b):
    return jnp.dot(a, b, preferred_element_type=F32)


def _dot_nt(a, b):
    return lax.dot_general(a, b, (((1,), (1,)), ((), ())), preferred_element_type=F32)


def _shift_rows(cur, carry_ref):
    rolled = pltpu.roll(cur, 1, axis=0)
    row = lax.broadcasted_iota(jnp.int32, cur.shape, 0)
    prev = jnp.where(row == 0, carry_ref[SUBLANES - 1:SUBLANES, :], rolled)
    carry_ref[...] = cur[cur.shape[0] - SUBLANES:, :]
    return prev


def _front_kernel(x_ref, ng_ref, w_in_ref, bg_ref, mu_rkv_ref, mu_lora_ref,
                  w0_ref, w1_ref, w2_ref, a0_ref, a1_ref, a2_ref, g1_ref, g2_ref,
                  r_ref, k_ref, v_ref, lw_ref, a_ref, g_ref, u_ref, ga_ref, gc_ref,
                  hcar_ref, pcar_ref):
    d = x_ref.shape[-1]

    @pl.when(pl.program_id(1) == 0)
    def _():
        hcar_ref[...] = jnp.zeros_like(hcar_ref)
        pcar_ref[...] = jnp.zeros_like(pcar_ref)

    x = x_ref[...]
    h = x * lax.rsqrt(jnp.mean(x * x, axis=-1, keepdims=True) + RMS_EPS) * ng_ref[...]
    hb = h.astype(BF16)

    for c, o_ref in enumerate((r_ref, k_ref, v_ref)):
        p = _dot(hb, w_in_ref[:, c * d:(c + 1) * d])
        p_prev = _shift_rows(p, pcar_ref.at[c])
        o_ref[...] = (p + (p_prev - p) * mu_rkv_ref[:, c * d:(c + 1) * d]).astype(o_ref.dtype)

    pa = _dot(hb, w_in_ref[:, 3 * d:4 * d])
    pb = _dot(hb, w_in_ref[:, 4 * d:5 * d])
    u_ref[...] = (pa * jax.nn.sigmoid(pb)).astype(u_ref.dtype)

    for c, o_ref in enumerate((ga_ref, gc_ref)):
        p = _dot(hb, w_in_ref[:, (5 + c) * d:(6 + c) * d]) + bg_ref[:, c * d:(c + 1) * d]
        o_ref[...] = jax.nn.sigmoid(p).astype(o_ref.dtype)

    dh = _shift_rows(h, hcar_ref) - h
    xw = (h + dh * mu_lora_ref[0:1, :]).astype(BF16)
    xa = (h + dh * mu_lora_ref[1:2, :]).astype(BF16)
    xg = (h + dh * mu_lora_ref[2:3, :]).astype(BF16)
    zw = w0_ref[...] + _dot(jnp.tanh(_dot(xw, w1_ref[...])).astype(BF16), w2_ref[...])
    lw_ref[...] = -DECAY_SCALE * jax.nn.sigmoid(zw)
    za = a0_ref[...] + _dot(_dot(xa, a1_ref[...]).astype(BF16), a2_ref[...])
    a_ref[...] = jax.nn.sigmoid(za).astype(a_ref.dtype)
    g_ref[...] = _dot(jax.nn.sigmoid(_dot(xg, g1_ref[...])).astype(BF16), g2_ref[...]).astype(g_ref.dtype)


def _front(x, ng, w_in, bg, mu_rkv, mu_lora, w0, w1, w2, a0, a1, a2, g1, g2, *, tm):
    b, t, d = x.shape
    assert t % tm == 0
    tile = pl.BlockSpec((None, tm, d), lambda i, j: (i, j, 0))

    def full(arr, single=False):
        kw = {"pipeline_mode": pl.Buffered(1)} if single else {}
        return pl.BlockSpec(arr.shape, lambda i, j: (0,) * arr.ndim, **kw)

    ins = (x, ng, w_in, bg, mu_rkv, mu_lora, w0, w1, w2, a0, a1, a2, g1, g2)
    in_specs = [tile, full(ng), full(w_in, single=True)] + [full(z) for z in ins[3:]]
    act = jax.ShapeDtypeStruct((b, t, d), BF16)
    out_shape = (act, act, act, jax.ShapeDtypeStruct((b, t, d), F32), act, act, act, act, act)
    return pl.pallas_call(
        _front_kernel,
        grid=(b, t // tm),
        in_specs=in_specs,
        out_specs=[tile] * len(out_shape),
        out_shape=out_shape,
        scratch_shapes=[pltpu.VMEM((SUBLANES, d), F32), pltpu.VMEM((3, SUBLANES, d), F32)],
        compiler_params=pltpu.CompilerParams(
            dimension_semantics=("arbitrary", "arbitrary"), vmem_limit_bytes=VMEM_LIMIT_BYTES),
        name="front",
    )(*ins)


def _cumsum_rows(x):
    n = x.shape[0]
    row = lax.broadcasted_iota(jnp.int32, x.shape, 0)
    s = 1
    while s < n:
        x = x + jnp.where(row >= s, pltpu.roll(x, s, axis=0), 0.0)
        s *= 2
    return x


def _wkv_kernel(r_ref, k_ref, v_ref, lw_ref, a_ref, g_ref,
                kk_ref, ka_ref, rk_ref, lng_ref, lnb_ref,
                o_ref, z_ref):
    c = WKV_CHUNK
    n = HEAD_SIZE
    tt = r_ref.shape[0]

    @pl.when(pl.program_id(2) == 0)
    def _():
        z_ref[...] = jnp.zeros_like(z_ref)

    row = lax.broadcasted_iota(jnp.int32, (c, c), 0)
    col = lax.broadcasted_iota(jnp.int32, (c, c), 1)
    strict = row > col
    incl = row >= col
    eye = (row == col).astype(F32)

    def chunk_body(ci, carry):
        rows = pl.ds(pl.multiple_of(ci * c, c), c)
        r = r_ref[rows, :].astype(F32)
        k = k_ref[rows, :].astype(F32)
        v = v_ref[rows, :].astype(F32)
        a_lr = a_ref[rows, :].astype(F32)
        lw = lw_ref[rows, :]
        gate = g_ref[rows, :].astype(F32)

        kkraw = k * kk_ref[...]
        kmod = k * (1.0 + (a_lr - 1.0) * ka_ref[...])
        cum = _cumsum_rows(lw)
        cum_last = cum[c - 1:c, :]
        e_in = jnp.exp(cum)
        e_ex = jnp.exp(cum - lw)
        e_neg = jnp.exp(-cum)
        e_rem = jnp.exp(cum_last - cum)
        w_chunk = jnp.exp(cum_last)
        rkr = r * kmod * rk_ref[...]

        for hh in range(HEADS_PER_GROUP):
            ls = slice(hh * n, (hh + 1) * n)
            kk_h = kkraw[:, ls]
            nrm = jnp.sqrt(jnp.sum(kk_h * kk_h, axis=-1, keepdims=True))
            kk_h = kk_h / jnp.maximum(nrm, 1e-12)
            b_h = kk_h * a_lr[:, ls]
            at = (-kk_h * e_ex[:, ls]).astype(BF16)
            rt = (r[:, ls] * e_in[:, ls]).astype(BF16)
            bt = (b_h * e_neg[:, ls]).astype(BF16)
            kt = (kmod[:, ls] * e_neg[:, ls]).astype(BF16)
            bh = b_h * e_rem[:, ls]
            kh = kmod[:, ls] * e_rem[:, ls]
            v_h = v[:, ls]
            vb = v_h.astype(BF16)

            a_ab = jnp.where(strict, _dot_nt(at, bt), 0.0)
            a_ak = jnp.where(strict, _dot_nt(at, kt), 0.0)
            a_rb = jnp.where(incl, _dot_nt(rt, bt), 0.0).astype(BF16)
            a_rk = jnp.where(incl, _dot_nt(rt, kt), 0.0).astype(BF16)

            tm = eye + a_ab
            pw = a_ab
            for _ in range(5):
                pwb = pw.astype(BF16)
                pw = _dot(pwb, pwb)
                tm = tm + _dot(tm.astype(BF16), pw.astype(BF16))
            tmb = tm.astype(BF16)

            a_hat = _dot(tmb, at)
            u_hat = _dot(tmb, _dot(a_ak.astype(BF16), vb).astype(BF16))
            a_hat_b = a_hat.astype(BF16)
            u_hat_b = u_hat.astype(BF16)
            q_hat = rt.astype(F32) + _dot(a_rb, a_hat_b)
            o_in = _dot(a_rb, u_hat_b) + _dot(a_rk, vb)
            bh_t = bh.T.astype(BF16)
            kh_t = kh.T.astype(BF16)
            m_mat = _dot(bh_t, a_hat_b)
            g_mat = _dot(bh_t, u_hat_b) + _dot(kh_t, vb)

            z = z_ref[hh]
            zb = z.astype(BF16)
            o_h = _dot(q_hat.astype(BF16), zb) + o_in
            w_col = w_chunk[:, ls].T
            z_ref[hh] = w_col * z + _dot(m_mat.astype(BF16), zb) + g_mat

            mu = jnp.mean(o_h, axis=-1, keepdims=True)
            dev = o_h - mu
            var = jnp.mean(dev * dev, axis=-1, keepdims=True)
            y = dev * lax.rsqrt(var + GN_EPS) * lng_ref[:, ls] + lnb_ref[:, ls]
            y = y + jnp.sum(rkr[:, ls], axis=-1, keepdims=True) * v_h
            o_ref[rows, ls] = (y * gate[:, ls]).astype(o_ref.dtype)
        return carry

    lax.fori_loop(0, tt // c, chunk_body, 0)


def _wkv(r, k, v, lw, a, g, k_k, k_a, r_k, ln_g, ln_b, *, tt):
    b, t, d = r.shape
    gl = HEADS_PER_GROUP * HEAD_SIZE
    assert t % tt == 0 and tt % WKV_CHUNK == 0 and d % gl == 0
    tile = pl.BlockSpec((None, tt, gl), lambda i, hg, j: (i, j, hg))
    vec = pl.BlockSpec((1, gl), lambda i, hg, j: (0, hg))
    return pl.pallas_call(
        _wkv_kernel,
        grid=(b, d // gl, t // tt),
        in_specs=[tile] * 6 + [vec] * 5,
        out_specs=tile,
        out_shape=jax.ShapeDtypeStruct((b, t, d), BF16),
        scratch_shapes=[pltpu.VMEM((HEADS_PER_GROUP, HEAD_SIZE, HEAD_SIZE), F32)],
        compiler_params=pltpu.CompilerParams(
            dimension_semantics=("arbitrary", "arbitrary", "arbitrary"), vmem_limit_bytes=VMEM_LIMIT_BYTES),
        name="wkv",
    )(r, k, v, lw, a, g, k_k, k_a, r_k, ln_g, ln_b)


def _conv_kernel(u_ref, w_ref, cb_ref, lng_ref, lnb_ref, o_ref, buf_ref, acc_ref):
    tm = u_ref.shape[0]
    rb = 64
    lb = 256

    @pl.when(pl.program_id(1) == 0)
    def _():
        buf_ref[0:CONV_HALO, :] = jnp.zeros((CONV_HALO, buf_ref.shape[1]), F32)

    buf_ref[CONV_HALO:, :] = u_ref[...].astype(F32)
    off = CONV_HALO - (CONV_WIDTH - 1)

    def block(bi, carry):
        r0 = pl.multiple_of(bi * rb, rb)
        for l0 in range(0, buf_ref.shape[1], lb):
            win = buf_ref[pl.ds(r0, rb + CONV_HALO), l0:l0 + lb]
            part = jnp.zeros((rb, lb), F32) + cb_ref[:, l0:l0 + lb]
            for j in range(CONV_WIDTH):
                part = part + win[off + j:off + j + rb, :] * w_ref[j:j + 1, l0:l0 + lb]
            acc_ref[pl.ds(r0, rb), l0:l0 + lb] = part
        acc = acc_ref[pl.ds(r0, rb), :]
        mu = jnp.mean(acc, axis=-1, keepdims=True)
        dev = acc - mu
        var = jnp.mean(dev * dev, axis=-1, keepdims=True)
        y = dev * lax.rsqrt(var + LN_EPS) * lng_ref[...] + lnb_ref[...]
        o_ref[pl.ds(r0, rb), :] = (y * jax.nn.sigmoid(y)).astype(o_ref.dtype)
        return carry

    lax.fori_loop(0, tm // rb, block, 0)
    buf_ref[0:CONV_HALO, :] = buf_ref[tm:tm + CONV_HALO, :]


def _conv(u, conv_w, conv_b, ln_g, ln_b, *, tm):
    b, t, d = u.shape
    assert t % tm == 0 and tm % 64 == 0
    tile = pl.BlockSpec((None, tm, d), lambda i, j: (i, j, 0))

    def full(arr):
        return pl.BlockSpec(arr.shape, lambda i, j: (0,) * arr.ndim)

    return pl.pallas_call(
        _conv_kernel,
        grid=(b, t // tm),
        in_specs=[tile, full(conv_w), full(conv_b), full(ln_g), full(ln_b)],
        out_specs=tile,
        out_shape=jax.ShapeDtypeStruct((b, t, d), BF16),
        scratch_shapes=[pltpu.VMEM((tm + CONV_HALO, d), F32), pltpu.VMEM((tm, d), F32)],
        compiler_params=pltpu.CompilerParams(
            dimension_semantics=("arbitrary", "arbitrary"), vmem_limit_bytes=VMEM_LIMIT_BYTES),
        name="conv",
    )(u, conv_w, conv_b, ln_g, ln_b)


def _back_kernel(x_ref, a_ref, c_ref, ga_ref, gc_ref, wa_ref, wc_ref, wo_ref,
                 nff_ref, w1_ref, w2_ref, nfin_ref, o_ref):
    d = x_ref.shape[-1]
    dff = w1_ref.shape[1]
    merged = (ga_ref[...].astype(F32) * _dot(a_ref[...], wa_ref[...])
              + gc_ref[...].astype(F32) * _dot(c_ref[...], wc_ref[...]))
    x1 = x_ref[...] + _dot(merged.astype(BF16), wo_ref[...])
    h2 = (x1 * lax.rsqrt(jnp.mean(x1 * x1, axis=-1, keepdims=True) + RMS_EPS) * nff_ref[...]).astype(BF16)
    acc = x1
    for c0 in range(0, dff, d):
        hid = jnp.maximum(_dot(h2, w1_ref[:, c0:c0 + d]), 0.0)
        acc = acc + _dot((hid * hid).astype(BF16), w2_ref[c0:c0 + d, :])
    o_ref[...] = acc * lax.rsqrt(jnp.mean(acc * acc, axis=-1, keepdims=True) + RMS_EPS) * nfin_ref[...]


def _back(x, a, c, ga, gc, wa, wc, wo, nff, w1, w2, nfin, *, tm):
    b, t, d = x.shape
    assert t % tm == 0
    tile = pl.BlockSpec((None, tm, d), lambda i, j: (i, j, 0))

    def full(arr, single=False):
        kw = {"pipeline_mode": pl.Buffered(1)} if single else {}
        return pl.BlockSpec(arr.shape, lambda i, j: (0,) * arr.ndim, **kw)

    return pl.pallas_call(
        _back_kernel,
        grid=(b, t // tm),
        in_specs=[tile] * 5 + [full(wa, True), full(wc, True), full(wo, True), full(nff),
                               full(w1, True), full(w2, True), full(nfin)],
        out_specs=tile,
        out_shape=jax.ShapeDtypeStruct((b, t, d), F32),
        compiler_params=pltpu.CompilerParams(
            dimension_semantics=("arbitrary", "arbitrary"), vmem_limit_bytes=VMEM_LIMIT_BYTES),
        name="back",
    )(x, a, c, ga, gc, wa, wc, wo, nff, w1, w2, nfin)


def _tile(t, want):
    while t % want:
        want //= 2
    return want


def kernel(x, norm_mix_g, w_in, b_gate, mu_rkv, mu_lora, decay_w0, decay_w1, decay_w2, aaa_a0, aaa_a1, aaa_a2, gate_g1, gate_g2, k_k, k_a, r_k, ln_x_g, ln_x_b, w_rwkv_proj, conv_w, conv_b, conv_ln_g, conv_ln_b, w_conv_proj, w_out, norm_ff_g, w_ff1, w_ff2, norm_final_g):
    depth = w_in.shape[0]
    t = x.shape[1]
    bf = lambda z: z.astype(BF16)
    row = lambda z: z.reshape(1, -1)
    for l in range(depth):
        r, k, v, lw, a, g, u, ga, gc = _front(
            x, row(norm_mix_g[l]), bf(w_in[l]), row(b_gate[l]), row(mu_rkv[l]), mu_lora[l],
            row(decay_w0[l]), bf(decay_w1[l]), bf(decay_w2[l]),
            row(aaa_a0[l]), bf(aaa_a1[l]), bf(aaa_a2[l]), bf(gate_g1[l]), bf(gate_g2[l]),
            tm=_tile(t, 256))
        a_out = _wkv(r, k, v, lw, a, g, row(k_k[l]), row(k_a[l]), row(r_k[l]), row(ln_x_g[l]), row(ln_x_b[l]),
                     tt=_tile(t, 512))
        c_out = _conv(u, conv_w[l], row(conv_b[l]), row(conv_ln_g[l]), row(conv_ln_b[l]), tm=_tile(t, 512))
        last = l == depth - 1
        nfin = row(norm_final_g) if last else None
        assert last, "only the final layer applies the closing norm"
        x = _back(x, a_out, c_out, ga, gc, bf(w_rwkv_proj[l]), bf(w_conv_proj[l]), bf(w_out[l]),
                  row(norm_ff_g[l]), bf(w_ff1[l]), bf(w_ff2[l]), nfin, tm=_tile(t, 256))
    return x
```

```python
import math

import jax
import jax.numpy as jnp
from jax import lax
from jax.experimental import pallas as pl
from jax.experimental.pallas import tpu as pltpu

HEAD_SIZE = 64
WKV_CHUNK = 64
HEADS_PER_GROUP = 4
GROUP_LANES = HEADS_PER_GROUP * HEAD_SIZE
PASS_A_CHUNKS = 2
CONV_WIDTH = 31
CONV_HALO = 32
RMS_EPS = 1e-6
LN_EPS = 1e-5
GN_EPS = 64e-5
DECAY_SCALE = math.exp(-0.5)
SUBLANES = 8

VMEM_LIMIT_BYTES = 56 * 1024 * 1024

F32 = jnp.float32
BF16 = jnp.bfloat16


def _dot(a, b):
    return jnp.dot(a, b, preferred_element_type=F32)


def _dot_nt(a, b):
    return lax.dot_general(a, b, (((1,), (1,)), ((), ())), preferred_element_type=F32)


def _shift_rows(cur, carry_ref):
    rolled = pltpu.roll(cur, 1, axis=0)
    row = lax.broadcasted_iota(jnp.int32, cur.shape, 0)
    prev = jnp.where(row == 0, carry_ref[SUBLANES - 1:SUBLANES, :], rolled)
    carry_ref[...] = cur[cur.shape[0] - SUBLANES:, :]
    return prev


def _front_kernel(x_ref, ng_ref, w_in_ref, bg_ref, mu_rkv_ref, mu_lora_ref,
                  w0_ref, w1_ref, w2_ref, a0_ref, a1_ref, a2_ref, g1_ref, g2_ref,
                  r_ref, k_ref, v_ref, lw_ref, a_ref, g_ref, u_ref, ga_ref, gc_ref,
                  hcar_ref, pcar_ref):
    d = x_ref.shape[-1]

    @pl.when(pl.program_id(1) == 0)
    def _():
        hcar_ref[...] = jnp.zeros_like(hcar_ref)
        pcar_ref[...] = jnp.zeros_like(pcar_ref)

    x = x_ref[...]
    h = x * lax.rsqrt(jnp.mean(x * x, axis=-1, keepdims=True) + RMS_EPS) * ng_ref[...]
    hb = h.astype(BF16)

    for c, o_ref in enumerate((r_ref, k_ref, v_ref)):
        p = _dot(hb, w_in_ref[:, c * d:(c + 1) * d])
        p_prev = _shift_rows(p, pcar_ref.at[c])
        o_ref[...] = (p + (p_prev - p) * mu_rkv_ref[:, c * d:(c + 1) * d]).astype(o_ref.dtype)

    pa = _dot(hb, w_in_ref[:, 3 * d:4 * d])
    pb = _dot(hb, w_in_ref[:, 4 * d:5 * d])
    u_ref[...] = (pa * jax.nn.sigmoid(pb)).astype(u_ref.dtype)

    for c, o_ref in enumerate((ga_ref, gc_ref)):
        p = _dot(hb, w_in_ref[:, (5 + c) * d:(6 + c) * d]) + bg_ref[:, c * d:(c + 1) * d]
        o_ref[...] = jax.nn.sigmoid(p).astype(o_ref.dtype)

    dh = _shift_rows(h, hcar_ref) - h
    xw = (h + dh * mu_lora_ref[0:1, :]).astype(BF16)
    xa = (h + dh * mu_lora_ref[1:2, :]).astype(BF16)
    xg = (h + dh * mu_lora_ref[2:3, :]).astype(BF16)
    zw = w0_ref[...] + _dot(jnp.tanh(_dot(xw, w1_ref[...])).astype(BF16), w2_ref[...])
    lw_ref[...] = -DECAY_SCALE * jax.nn.sigmoid(zw)
    za = a0_ref[...] + _dot(_dot(xa, a1_ref[...]).astype(BF16), a2_ref[...])
    a_ref[...] = jax.nn.sigmoid(za).astype(a_ref.dtype)
    g_ref[...] = _dot(jax.nn.sigmoid(_dot(xg, g1_ref[...])).astype(BF16), g2_ref[...]).astype(g_ref.dtype)


def _front(x, ng, w_in, bg, mu_rkv, mu_lora, w0, w1, w2, a0, a1, a2, g1, g2, *, tm):
    b, t, d = x.shape
    assert t % tm == 0
    tile = pl.BlockSpec((None, tm, d), lambda i, j: (i, j, 0))

    def full(arr, single=False):
        kw = {"pipeline_mode": pl.Buffered(1)} if single else {}
        return pl.BlockSpec(arr.shape, lambda i, j: (0,) * arr.ndim, **kw)

    ins = (x, ng, w_in, bg, mu_rkv, mu_lora, w0, w1, w2, a0, a1, a2, g1, g2)
    in_specs = [tile, full(ng), full(w_in, single=True)] + [full(z) for z in ins[3:]]
    act = jax.ShapeDtypeStruct((b, t, d), BF16)
    out_shape = (act, act, act, jax.ShapeDtypeStruct((b, t, d), F32), act, act, act, act, act)
    return pl.pallas_call(
        _front_kernel,
        grid=(b, t // tm),
        in_specs=in_specs,
        out_specs=[tile] * len(out_shape),
        out_shape=out_shape,
        scratch_shapes=[pltpu.VMEM((SUBLANES, d), F32), pltpu.VMEM((3, SUBLANES, d), F32)],
        compiler_params=pltpu.CompilerParams(
            dimension_semantics=("arbitrary", "arbitrary"), vmem_limit_bytes=VMEM_LIMIT_BYTES),
        name="front",
    )(*ins)


def _cumsum_rows(x):
    n = x.shape[0]
    row = lax.broadcasted_iota(jnp.int32, x.shape, 0)
    s = 1
    while s < n:
        x = x + jnp.where(row >= s, pltpu.roll(x, s, axis=0), 0.0)
        s *= 2
    return x


def _head_sums(x):
    lane_head = lax.broadcasted_iota(jnp.int32, x.shape, 1) // HEAD_SIZE
    out = jnp.zeros_like(x)
    for hh in range(HEADS_PER_GROUP):
        sel = lane_head == hh
        s = jnp.sum(jnp.where(sel, x, 0.0), axis=-1, keepdims=True)
        out = jnp.where(sel, s, out)
    return out


def _bd(x, mask):
    return jnp.concatenate([x] * HEADS_PER_GROUP, axis=0) * mask


def _wkv_kernel(r_ref, k_ref, v_ref, lw_ref, a_ref, g_ref,
                kk_ref, ka_ref, rk_ref, lng_ref, lnb_ref, bdm_ref,
                o_ref, z_ref, qa_ref, oin_ref, uh_ref, bkt_ref, lc_ref):
    c = WKV_CHUNK
    gl = GROUP_LANES
    n_groups = r_ref.shape[1] // gl
    n_chunks = r_ref.shape[0] // c

    @pl.when(pl.program_id(1) == 0)
    def _():
        z_ref[...] = jnp.zeros_like(z_ref)

    row = lax.broadcasted_iota(jnp.int32, (c, gl), 0)
    colj = lax.broadcasted_iota(jnp.int32, (c, gl), 1) % HEAD_SIZE
    strict = row > colj
    incl = row >= colj
    eye = (row == colj).astype(F32)
    bdm = bdm_ref[...]
    brow = lax.broadcasted_iota(jnp.int32, (gl, gl), 0) // HEAD_SIZE
    bcol = lax.broadcasted_iota(jnp.int32, (gl, gl), 1) // HEAD_SIZE
    bd_sel = brow == bcol

    cpi = PASS_A_CHUNKS

    def pass_a(it, carry):
        st = []
        for cc in range(cpi):
            ci = it * cpi + cc
            rows = pl.ds(pl.multiple_of(ci * c, c), c)
            for gi in range(n_groups):
                lanes = slice(gi * gl, (gi + 1) * gl)
                r = r_ref[rows, lanes].astype(F32)
                k = k_ref[rows, lanes].astype(F32)
                a_lr = a_ref[rows, lanes].astype(F32)
                lw = lw_ref[rows, lanes]
                kk = k * kk_ref[:, lanes]
                kk = kk / jnp.maximum(jnp.sqrt(_head_sums(kk * kk)), 1e-12)
                b = kk * a_lr
                kmod = k * (1.0 + (a_lr - 1.0) * ka_ref[:, lanes])
                cum = _cumsum_rows(lw)
                cum_last = cum[c - 1:c, :]
                e_neg = jnp.exp(-cum)
                e_rem = jnp.exp(cum_last - cum)
                at = (-kk * jnp.exp(cum - lw)).astype(BF16)
                rt = (r * jnp.exp(cum)).astype(BF16)
                bkt_ref[ci, lanes, :] = jnp.concatenate([b * e_rem, kmod * e_rem], axis=0).T.astype(BF16)
                lc_ref[ci, :, lanes] = jnp.broadcast_to(cum_last, (SUBLANES, gl))
                st.append(dict(ci=ci, lanes=lanes, at=at, rt=rt, lhs=jnp.concatenate([at, rt], axis=0),
                               bt=(b * e_neg).astype(BF16), kt=(kmod * e_neg).astype(BF16),
                               vbd=_bd(v_ref[rows, lanes], bdm)))
        for s in st:
            s["sb"] = _dot_nt(s["lhs"], _bd(s["bt"], bdm))
        for s in st:
            s["sk"] = _dot_nt(s["lhs"], _bd(s["kt"], bdm))
        for s in st:
            p = jnp.where(strict, s["sb"][:c], 0.0)
            s["tm"] = eye + p
            pb = p.astype(BF16)
            s["p"] = _dot(pb, _bd(pb, bdm))
        for _ in range(4):
            for s in st:
                pb = s["p"].astype(BF16)
                res = _dot(jnp.concatenate([s["tm"].astype(BF16), pb], axis=0), _bd(pb, bdm))
                s["tm"] = s["tm"] + res[:c]
                s["p"] = res[c:]
        for s in st:
            tm = s["tm"] + _dot(s["tm"].astype(BF16), _bd(s["p"].astype(BF16), bdm))
            s["tmb"] = tm.astype(BF16)
        for s in st:
            ak = jnp.where(strict, s["sk"][:c], 0.0).astype(BF16)
            s["akv"] = _dot(ak, s["vbd"])
        for s in st:
            s["a_hat"] = _dot(s["tmb"], _bd(s["at"], bdm)).astype(BF16)
        for s in st:
            s["u_hat"] = _dot(s["tmb"], _bd(s["akv"].astype(BF16), bdm))
        for s in st:
            s["rb"] = jnp.where(incl, s["sb"][c:], 0.0).astype(BF16)
            s["q_hat"] = s["rt"].astype(F32) + _dot(s["rb"], _bd(s["a_hat"], bdm))
        for s in st:
            rk = jnp.where(incl, s["sk"][c:], 0.0).astype(BF16)
            s["o_in"] = _dot(s["rb"], _bd(s["u_hat"].astype(BF16), bdm)) + _dot(rk, s["vbd"])
        for s in st:
            ci, lanes = s["ci"], s["lanes"]
            qa_ref[ci, 0:c, lanes] = s["q_hat"].astype(BF16)
            qa_ref[ci, c:2 * c, lanes] = s["a_hat"]
            oin_ref[ci, :, lanes] = s["o_in"]
            uh_ref[ci, :, lanes] = s["u_hat"]
        return carry

    lax.fori_loop(0, n_chunks // cpi, pass_a, 0)

    def pass_b(ci, carry):
        rows = pl.ds(pl.multiple_of(ci * c, c), c)
        groups = [slice(gi * gl, (gi + 1) * gl) for gi in range(n_groups)]
        zs = [z_ref[gi] for gi in range(n_groups)]
        res = [_dot(qa_ref[ci, :, lanes], z.astype(BF16)) for lanes, z in zip(groups, zs)]
        outs = [rs[:c] + oin_ref[ci, :, lanes] for lanes, rs in zip(groups, res)]
        upd = [_dot(bkt_ref[ci, lanes, :],
                    jnp.concatenate([(rs[c:] + uh_ref[ci, :, lanes]).astype(BF16), v_ref[rows, lanes]], axis=0))
               for lanes, rs in zip(groups, res)]
        for gi, lanes in enumerate(groups):
            w_col = jnp.exp(jnp.broadcast_to(lc_ref[ci, 0:1, lanes], (2 * c, gl)).T)
            z_ref[gi] = jnp.concatenate([w_col, w_col], axis=1) * zs[gi] + jnp.where(bd_sel, upd[gi], 0.0)
        for lanes, o in zip(groups, outs):
            r = r_ref[rows, lanes].astype(F32)
            k = k_ref[rows, lanes].astype(F32)
            a_lr = a_ref[rows, lanes].astype(F32)
            kmod = k * (1.0 + (a_lr - 1.0) * ka_ref[:, lanes])
            mu = _head_sums(o) * (1.0 / HEAD_SIZE)
            dev = o - mu
            var = _head_sums(dev * dev) * (1.0 / HEAD_SIZE)
            y = dev * lax.rsqrt(var + GN_EPS) * lng_ref[:, lanes] + lnb_ref[:, lanes]
            y = y + _head_sums(r * kmod * rk_ref[:, lanes]) * v_ref[rows, lanes].astype(F32)
            o_ref[rows, lanes] = (y * g_ref[rows, lanes].astype(F32)).astype(o_ref.dtype)
        return carry

    lax.fori_loop(0, n_chunks, pass_b, 0)


def _wkv(r, k, v, lw, a, g, k_k, k_a, r_k, ln_g, ln_b, *, tt):
    b, t, d = r.shape
    gl = GROUP_LANES
    c = WKV_CHUNK
    assert t % tt == 0 and tt % c == 0 and d % gl == 0
    n_chunks = tt // c
    head_of = jnp.arange(gl) // HEAD_SIZE
    bd_mask = (head_of[:, None] == head_of[None, :]).astype(BF16)
    tile = pl.BlockSpec((None, tt, d), lambda i, j: (i, j, 0))
    vec = pl.BlockSpec((1, d), lambda i, j: (0, 0))
    return pl.pallas_call(
        _wkv_kernel,
        grid=(b, t // tt),
        in_specs=[tile] * 6 + [vec] * 5 + [pl.BlockSpec((gl, gl), lambda i, j: (0, 0))],
        out_specs=tile,
        out_shape=jax.ShapeDtypeStruct((b, t, d), BF16),
        scratch_shapes=[
            pltpu.VMEM((d // gl, gl, gl), F32),
            pltpu.VMEM((n_chunks, 2 * c, d), BF16),
            pltpu.VMEM((n_chunks, c, d), F32),
            pltpu.VMEM((n_chunks, c, d), F32),
            pltpu.VMEM((n_chunks, d, 2 * c), BF16),
            pltpu.VMEM((n_chunks, SUBLANES, d), F32),
        ],
        compiler_params=pltpu.CompilerParams(
            dimension_semantics=("arbitrary", "arbitrary"), vmem_limit_bytes=VMEM_LIMIT_BYTES),
        name="wkv",
    )(r, k, v, lw, a, g, k_k, k_a, r_k, ln_g, ln_b, bd_mask)


def _conv_kernel(u_ref, w_ref, cb_ref, lng_ref, lnb_ref, o_ref, buf_ref, acc_ref):
    tm = u_ref.shape[0]
    rb = 64
    lb = 256

    @pl.when(pl.program_id(1) == 0)
    def _():
        buf_ref[0:CONV_HALO, :] = jnp.zeros((CONV_HALO, buf_ref.shape[1]), F32)

    buf_ref[CONV_HALO:, :] = u_ref[...].astype(F32)
    off = CONV_HALO - (CONV_WIDTH - 1)

    def block(bi, carry):
        r0 = pl.multiple_of(bi * rb, rb)
        for l0 in range(0, buf_ref.shape[1], lb):
            win = buf_ref[pl.ds(r0, rb + CONV_HALO), l0:l0 + lb]
            part = jnp.zeros((rb, lb), F32) + cb_ref[:, l0:l0 + lb]
            for j in range(CONV_WIDTH):
                part = part + win[off + j:off + j + rb, :] * w_ref[j:j + 1, l0:l0 + lb]
            acc_ref[pl.ds(r0, rb), l0:l0 + lb] = part
        acc = acc_ref[pl.ds(r0, rb), :]
        mu = jnp.mean(acc, axis=-1, keepdims=True)
        dev = acc - mu
        var = jnp.mean(dev * dev, axis=-1, keepdims=True)
        y = dev * lax.rsqrt(var + LN_EPS) * lng_ref[...] + lnb_ref[...]
        o_ref[pl.ds(r0, rb), :] = (y * jax.nn.sigmoid(y)).astype(o_ref.dtype)
        return carry

    lax.fori_loop(0, tm // rb, block, 0)
    buf_ref[0:CONV_HALO, :] = buf_ref[tm:tm + CONV_HALO, :]


def _conv(u, conv_w, conv_b, ln_g, ln_b, *, tm):
    b, t, d = u.shape
    assert t % tm == 0 and tm % 64 == 0
    tile = pl.BlockSpec((None, tm, d), lambda i, j: (i, j, 0))

    def full(arr):
        return pl.BlockSpec(arr.shape, lambda i, j: (0,) * arr.ndim)

    return pl.pallas_call(
        _conv_kernel,
        grid=(b, t // tm),
        in_specs=[tile, full(conv_w), full(conv_b), full(ln_g), full(ln_b)],
        out_specs=tile,
        out_shape=jax.ShapeDtypeStruct((b, t, d), BF16),
        scratch_shapes=[pltpu.VMEM((tm + CONV_HALO, d), F32), pltpu.VMEM((tm, d), F32)],
        compiler_params=pltpu.CompilerParams(
            dimension_semantics=("arbitrary", "arbitrary"), vmem_limit_bytes=VMEM_LIMIT_BYTES),
        name="conv",
    )(u, conv_w, conv_b, ln_g, ln_b)


def _back_kernel(x_ref, a_ref, c_ref, ga_ref, gc_ref, wa_ref, wc_ref, wo_ref,
                 nff_ref, w1_ref, w2_ref, nfin_ref, o_ref):
    d = x_ref.shape[-1]
    dff = w1_ref.shape[1]
    merged = (ga_ref[...].astype(F32) * _dot(a_ref[...], wa_ref[...])
              + gc_ref[...].astype(F32) * _dot(c_ref[...], wc_ref[...]))
    x1 = x_ref[...] + _dot(merged.astype(BF16), wo_ref[...])
    h2 = (x1 * lax.rsqrt(jnp.mean(x1 * x1, axis=-1, keepdims=True) + RMS_EPS) * nff_ref[...]).astype(BF16)
    acc = x1
    for c0 in range(0, dff, d):
        hid = jnp.maximum(_dot(h2, w1_ref[:, c0:c0 + d]), 0.0)
        acc = acc + _dot((hid * hid).astype(BF16), w2_ref[c0:c0 + d, :])
    o_ref[...] = acc * lax.rsqrt(jnp.mean(acc * acc, axis=-1, keepdims=True) + RMS_EPS) * nfin_ref[...]


def _back(x, a, c, ga, gc, wa, wc, wo, nff, w1, w2, nfin, *, tm):
    b, t, d = x.shape
    assert t % tm == 0
    tile = pl.BlockSpec((None, tm, d), lambda i, j: (i, j, 0))

    def full(arr, single=False):
        kw = {"pipeline_mode": pl.Buffered(1)} if single else {}
        return pl.BlockSpec(arr.shape, lambda i, j: (0,) * arr.ndim, **kw)

    return pl.pallas_call(
        _back_kernel,
        grid=(b, t // tm),
        in_specs=[tile] * 5 + [full(wa, True), full(wc, True), full(wo, True), full(nff),
                               full(w1, True), full(w2, True), full(nfin)],
        out_specs=tile,
        out_shape=jax.ShapeDtypeStruct((b, t, d), F32),
        compiler_params=pltpu.CompilerParams(
            dimension_semantics=("arbitrary", "arbitrary"), vmem_limit_bytes=VMEM_LIMIT_BYTES),
        name="back",
    )(x, a, c, ga, gc, wa, wc, wo, nff, w1, w2, nfin)


def _tile(t, want):
    while t % want:
        want //= 2
    return want


def kernel(x, norm_mix_g, w_in, b_gate, mu_rkv, mu_lora, decay_w0, decay_w1, decay_w2, aaa_a0, aaa_a1, aaa_a2, gate_g1, gate_g2, k_k, k_a, r_k, ln_x_g, ln_x_b, w_rwkv_proj, conv_w, conv_b, conv_ln_g, conv_ln_b, w_conv_proj, w_out, norm_ff_g, w_ff1, w_ff2, norm_final_g):
    assert w_in.shape[0] == 1, "single trunk layer"
    t = x.shape[1]
    bf = lambda z: z[0].astype(BF16)
    row = lambda z: z.reshape(1, -1)
    r, k, v, lw, a, g, u, ga, gc = _front(
        x, row(norm_mix_g), bf(w_in), row(b_gate), row(mu_rkv), mu_lora[0],
        row(decay_w0), bf(decay_w1), bf(decay_w2),
        row(aaa_a0), bf(aaa_a1), bf(aaa_a2), bf(gate_g1), bf(gate_g2),
        tm=_tile(t, 256))
    a_out = _wkv(r, k, v, lw, a, g, row(k_k), row(k_a), row(r_k), row(ln_x_g), row(ln_x_b),
                 tt=_tile(t, 512))
    c_out = _conv(u, conv_w[0], row(conv_b), row(conv_ln_g), row(conv_ln_b), tm=_tile(t, 512))
    return _back(x, a_out, c_out, ga, gc, bf(w_rwkv_proj), bf(w_conv_proj), bf(w_out),
                 row(norm_ff_g), bf(w_ff1), bf(w_ff2), row(norm_final_g), tm=_tile(t, 256))
```

```python
import functools
import math

import jax
import jax.numpy as jnp
from jax import lax
from jax.experimental import pallas as pl
from jax.experimental.pallas import tpu as pltpu

HEAD_SIZE = 64
WKV_CHUNK = 64
HEADS_PER_GROUP = 4
GROUP_LANES = HEADS_PER_GROUP * HEAD_SIZE
PASS_A_CHUNKS = 4
CONV_WIDTH = 31
CONV_HALO = 32
RMS_EPS = 1e-6
LN_EPS = 1e-5
GN_EPS = 64e-5
DECAY_SCALE = math.exp(-0.5)
SUBLANES = 8
MXU_COLS = 256

VMEM_LIMIT_BYTES = 56 * 1024 * 1024

F32 = jnp.float32
BF16 = jnp.bfloat16


def _dot(a, b):
    return jnp.dot(a, b, preferred_element_type=F32)


def _dot_nt(a, b):
    return lax.dot_general(a, b, (((1,), (1,)), ((), ())), preferred_element_type=F32)


def _shift_rows(cur, carry_ref):
    rolled = pltpu.roll(cur, 1, axis=0)
    row = lax.broadcasted_iota(jnp.int32, cur.shape, 0)
    prev = jnp.where(row == 0, carry_ref[SUBLANES - 1:SUBLANES, :], rolled)
    carry_ref[...] = cur[cur.shape[0] - SUBLANES:, :]
    return prev


def _front_kernel(x_ref, ng_ref, w_in_ref, bg_ref, mu_rkv_ref, mu_lora_ref,
                  w0_ref, w1_ref, w2_ref, a0_ref, a1_ref, a2_ref, g1_ref, g2_ref,
                  r_ref, k_ref, v_ref, lw_ref, a_ref, g_ref, u_ref, ga_ref, gc_ref,
                  hcar_ref, pcar_ref):
    d = x_ref.shape[-1]

    @pl.when(pl.program_id(1) == 0)
    def _():
        hcar_ref[...] = jnp.zeros_like(hcar_ref)
        pcar_ref[...] = jnp.zeros_like(pcar_ref)

    x = x_ref[...]
    h = x * lax.rsqrt(jnp.mean(x * x, axis=-1, keepdims=True) + RMS_EPS) * ng_ref[...]
    hb = h.astype(BF16)

    for c, o_ref in enumerate((r_ref, k_ref, v_ref)):
        p = _dot(hb, w_in_ref[:, c * d:(c + 1) * d])
        p_prev = _shift_rows(p, pcar_ref.at[c])
        o_ref[...] = (p + (p_prev - p) * mu_rkv_ref[:, c * d:(c + 1) * d]).astype(o_ref.dtype)

    pa = _dot(hb, w_in_ref[:, 3 * d:4 * d])
    pb = _dot(hb, w_in_ref[:, 4 * d:5 * d])
    u_ref[...] = (pa * jax.nn.sigmoid(pb)).astype(u_ref.dtype)

    for c, o_ref in enumerate((ga_ref, gc_ref)):
        p = _dot(hb, w_in_ref[:, (5 + c) * d:(6 + c) * d]) + bg_ref[:, c * d:(c + 1) * d]
        o_ref[...] = jax.nn.sigmoid(p).astype(o_ref.dtype)

    dh = _shift_rows(h, hcar_ref) - h
    xw = (h + dh * mu_lora_ref[0:1, :]).astype(BF16)
    xa = (h + dh * mu_lora_ref[1:2, :]).astype(BF16)
    xg = (h + dh * mu_lora_ref[2:3, :]).astype(BF16)
    zw = w0_ref[...] + _dot(jnp.tanh(_dot(xw, w1_ref[...])).astype(BF16), w2_ref[...])
    lw_ref[...] = -DECAY_SCALE * jax.nn.sigmoid(zw)
    za = a0_ref[...] + _dot(_dot(xa, a1_ref[...]).astype(BF16), a2_ref[...])
    a_ref[...] = jax.nn.sigmoid(za).astype(a_ref.dtype)
    g_ref[...] = _dot(jax.nn.sigmoid(_dot(xg, g1_ref[...])).astype(BF16), g2_ref[...]).astype(g_ref.dtype)


def _front(x, ng, w_in, bg, mu_rkv, mu_lora, w0, w1, w2, a0, a1, a2, g1, g2, *, tm):
    b, t, d = x.shape
    assert t % tm == 0
    tile = pl.BlockSpec((None, tm, d), lambda i, j: (i, j, 0))

    def full(arr, single=False):
        kw = {"pipeline_mode": pl.Buffered(1)} if single else {}
        return pl.BlockSpec(arr.shape, lambda i, j: (0,) * arr.ndim, **kw)

    ins = (x, ng, w_in, bg, mu_rkv, mu_lora, w0, w1, w2, a0, a1, a2, g1, g2)
    in_specs = [tile, full(ng), full(w_in, single=True)] + [full(z) for z in ins[3:]]
    act = jax.ShapeDtypeStruct((b, t, d), BF16)
    out_shape = (act, act, act, jax.ShapeDtypeStruct((b, t, d), F32), act, act, act, act, act)
    return pl.pallas_call(
        _front_kernel,
        grid=(b, t // tm),
        in_specs=in_specs,
        out_specs=[tile] * len(out_shape),
        out_shape=out_shape,
        scratch_shapes=[pltpu.VMEM((SUBLANES, d), F32), pltpu.VMEM((3, SUBLANES, d), F32)],
        compiler_params=pltpu.CompilerParams(
            dimension_semantics=("arbitrary", "arbitrary"), vmem_limit_bytes=VMEM_LIMIT_BYTES),
        name="front",
    )(*ins)


def _cumsum_rows(x):
    n = x.shape[0]
    row = lax.broadcasted_iota(jnp.int32, x.shape, 0)
    s = 1
    while s < n:
        x = x + jnp.where(row >= s, pltpu.roll(x, s, axis=0), 0.0)
        s *= 2
    return x


def _bd(x, mask):
    return jnp.concatenate([x] * HEADS_PER_GROUP, axis=0) * mask


def _wkv_kernel(r_ref, k_ref, v_ref, lw_ref, a_ref, g_ref,
                kk_ref, ka_ref, rk_ref, lng_ref, lnb_ref, bdm_ref,
                o_ref, z_ref, qa_ref, oin_ref, uh_ref, bkt_ref, lc_ref):
    c = WKV_CHUNK
    gl = GROUP_LANES
    n_groups = r_ref.shape[1] // gl
    n_chunks = r_ref.shape[0] // c

    @pl.when(pl.program_id(1) == 0)
    def _():
        z_ref[...] = jnp.zeros_like(z_ref)

    row = lax.broadcasted_iota(jnp.int32, (c, gl), 0)
    colj = lax.broadcasted_iota(jnp.int32, (c, gl), 1) % HEAD_SIZE
    strict = row > colj
    incl = row >= colj
    eye = (row == colj).astype(F32)
    bdm = bdm_ref[...]
    brow = lax.broadcasted_iota(jnp.int32, (gl, gl), 0) // HEAD_SIZE
    bcol = lax.broadcasted_iota(jnp.int32, (gl, gl), 1) // HEAD_SIZE
    bd_sel = brow == bcol

    cpi = PASS_A_CHUNKS

    def pass_a(it, carry):
        st = []
        chains = []
        for cc in range(cpi):
            ci = it * cpi + cc
            rows = pl.ds(pl.multiple_of(ci * c, c), c)
            for gi in range(n_groups):
                lanes = slice(gi * gl, (gi + 1) * gl)
                k = k_ref[rows, lanes].astype(F32)
                chains.append((ci, rows, lanes, k, k * kk_ref[:, lanes]))
        sq_sums = _dot(jnp.concatenate([(kk * kk).astype(BF16) for *_, kk in chains], axis=0), bdm)
        for idx, (ci, rows, lanes, k, kk) in enumerate(chains):
            r = r_ref[rows, lanes].astype(F32)
            a_lr = a_ref[rows, lanes].astype(F32)
            lw = lw_ref[rows, lanes]
            kk = kk / jnp.maximum(jnp.sqrt(sq_sums[idx * c:(idx + 1) * c]), 1e-12)
            b = kk * a_lr
            kmod = k * (1.0 + (a_lr - 1.0) * ka_ref[:, lanes])
            cum = _cumsum_rows(lw)
            cum_last = cum[c - 1:c, :]
            e_neg = jnp.exp(-cum)
            e_rem = jnp.exp(cum_last - cum)
            at = (-kk * jnp.exp(cum - lw)).astype(BF16)
            rt = (r * jnp.exp(cum)).astype(BF16)
            bkt_ref[ci, lanes, :] = jnp.concatenate([b * e_rem, kmod * e_rem], axis=0).T.astype(BF16)
            lc_ref[ci, :, lanes] = jnp.broadcast_to(cum_last, (SUBLANES, gl))
            st.append(dict(ci=ci, lanes=lanes, at=at, rt=rt, lhs=jnp.concatenate([at, rt], axis=0),
                           bt=(b * e_neg).astype(BF16), kt=(kmod * e_neg).astype(BF16),
                           vbd=_bd(v_ref[rows, lanes], bdm)))
        for s in st:
            s["sb"] = _dot_nt(s["lhs"], _bd(s["bt"], bdm))
        for s in st:
            s["sk"] = _dot_nt(s["lhs"], _bd(s["kt"], bdm))
        for s in st:
            p = jnp.where(strict, s["sb"][:c], 0.0)
            s["tm"] = eye + p
            pb = p.astype(BF16)
            s["p"] = _dot(pb, _bd(pb, bdm))
        for _ in range(4):
            for s in st:
                pb = s["p"].astype(BF16)
                res = _dot(jnp.concatenate([s["tm"].astype(BF16), pb], axis=0), _bd(pb, bdm))
                s["tm"] = s["tm"] + res[:c]
                s["p"] = res[c:]
        for s in st:
            tm = s["tm"] + _dot(s["tm"].astype(BF16), _bd(s["p"].astype(BF16), bdm))
            s["tmb"] = tm.astype(BF16)
        for s in st:
            ak = jnp.where(strict, s["sk"][:c], 0.0).astype(BF16)
            s["akv"] = _dot(ak, s["vbd"])
        for s in st:
            s["a_hat"] = _dot(s["tmb"], _bd(s["at"], bdm)).astype(BF16)
        for s in st:
            s["u_hat"] = _dot(s["tmb"], _bd(s["akv"].astype(BF16), bdm))
        for s in st:
            s["rb"] = jnp.where(incl, s["sb"][c:], 0.0).astype(BF16)
            s["q_hat"] = s["rt"].astype(F32) + _dot(s["rb"], _bd(s["a_hat"], bdm))
        for s in st:
            rk = jnp.where(incl, s["sk"][c:], 0.0).astype(BF16)
            s["o_in"] = _dot(s["rb"], _bd(s["u_hat"].astype(BF16), bdm)) + _dot(rk, s["vbd"])
        for s in st:
            ci, lanes = s["ci"], s["lanes"]
            qa_ref[ci, 0:c, lanes] = s["q_hat"].astype(BF16)
            qa_ref[ci, c:2 * c, lanes] = s["a_hat"]
            oin_ref[ci, :, lanes] = s["o_in"]
            uh_ref[ci, :, lanes] = s["u_hat"]
        return carry

    lax.fori_loop(0, n_chunks // cpi, pass_a, 0)

    def pass_b(ci, carry):
        rows = pl.ds(pl.multiple_of(ci * c, c), c)
        groups = [slice(gi * gl, (gi + 1) * gl) for gi in range(n_groups)]
        zs = [z_ref[gi] for gi in range(n_groups)]
        res = [_dot(qa_ref[ci, :, lanes], z.astype(BF16)) for lanes, z in zip(groups, zs)]
        outs = [rs[:c] + oin_ref[ci, :, lanes] for lanes, rs in zip(groups, res)]
        upd = [_dot(bkt_ref[ci, lanes, :],
                    jnp.concatenate([(rs[c:] + uh_ref[ci, :, lanes]).astype(BF16), v_ref[rows, lanes]], axis=0))
               for lanes, rs in zip(groups, res)]
        for gi, lanes in enumerate(groups):
            w_col = jnp.exp(jnp.broadcast_to(lc_ref[ci, 0:1, lanes], (2 * c, gl)).T)
            z_ref[gi] = jnp.concatenate([w_col, w_col], axis=1) * zs[gi] + jnp.where(bd_sel, upd[gi], 0.0)
        rkr = []
        for lanes in groups:
            k = k_ref[rows, lanes].astype(F32)
            kmod = k * (1.0 + (a_ref[rows, lanes].astype(F32) - 1.0) * ka_ref[:, lanes])
            rkr.append(r_ref[rows, lanes].astype(F32) * kmod * rk_ref[:, lanes])
        sums = _dot(jnp.concatenate([x.astype(BF16) for x in outs + rkr], axis=0), bdm)
        devs = [o - sums[gi * c:(gi + 1) * c] * (1.0 / HEAD_SIZE) for gi, o in enumerate(outs)]
        sq_sums = _dot(jnp.concatenate([(dv * dv).astype(BF16) for dv in devs], axis=0), bdm)
        for gi, lanes in enumerate(groups):
            var = sq_sums[gi * c:(gi + 1) * c] * (1.0 / HEAD_SIZE)
            y = devs[gi] * lax.rsqrt(var + GN_EPS) * lng_ref[:, lanes] + lnb_ref[:, lanes]
            y = y + sums[(n_groups + gi) * c:(n_groups + gi + 1) * c] * v_ref[rows, lanes].astype(F32)
            o_ref[rows, lanes] = (y * g_ref[rows, lanes].astype(F32)).astype(o_ref.dtype)
        return carry

    lax.fori_loop(0, n_chunks, pass_b, 0)


def _wkv(r, k, v, lw, a, g, k_k, k_a, r_k, ln_g, ln_b, *, tt):
    b, t, d = r.shape
    gl = GROUP_LANES
    c = WKV_CHUNK
    assert t % tt == 0 and tt % c == 0 and d % gl == 0
    n_chunks = tt // c
    head_of = jnp.arange(gl) // HEAD_SIZE
    bd_mask = (head_of[:, None] == head_of[None, :]).astype(BF16)
    tile = pl.BlockSpec((None, tt, d), lambda i, j: (i, j, 0))
    vec = pl.BlockSpec((1, d), lambda i, j: (0, 0))
    return pl.pallas_call(
        _wkv_kernel,
        grid=(b, t // tt),
        in_specs=[tile] * 6 + [vec] * 5 + [pl.BlockSpec((gl, gl), lambda i, j: (0, 0))],
        out_specs=tile,
        out_shape=jax.ShapeDtypeStruct((b, t, d), BF16),
        scratch_shapes=[
            pltpu.VMEM((d // gl, gl, gl), F32),
            pltpu.VMEM((n_chunks, 2 * c, d), BF16),
            pltpu.VMEM((n_chunks, c, d), F32),
            pltpu.VMEM((n_chunks, c, d), F32),
            pltpu.VMEM((n_chunks, d, 2 * c), BF16),
            pltpu.VMEM((n_chunks, SUBLANES, d), F32),
        ],
        compiler_params=pltpu.CompilerParams(
            dimension_semantics=("arbitrary", "arbitrary"), vmem_limit_bytes=VMEM_LIMIT_BYTES),
        name="wkv",
    )(r, k, v, lw, a, g, k_k, k_a, r_k, ln_g, ln_b, bd_mask)


CONV_ROWS = 64
CONV_LANES = 128


def _zero_of(v):
    bits = pltpu.bitcast(v[0:SUBLANES, :].astype(F32), jnp.uint32)
    sixteen = jnp.uint32(16)
    bits = lax.shift_right_logical(lax.shift_right_logical(bits, sixteen), sixteen)
    return pltpu.bitcast(bits, F32)[0:1, :]


def _conv_rows(r0, w_ref, cb_ref, lng_ref, lnb_ref, out_ref, buf_ref, acc_ref, tie=None):
    rb, lb = CONV_ROWS, CONV_LANES
    off = CONV_HALO - (CONV_WIDTH - 1)
    win_rows = rb + CONV_HALO
    bias = cb_ref[...] if tie is None else cb_ref[...] + tie
    for l0 in range(0, buf_ref.shape[1], lb):
        win = buf_ref[r0:r0 + win_rows, l0:l0 + lb]
        part = jnp.zeros((rb, lb), F32) + bias[:, l0:l0 + lb]
        for s in range(SUBLANES):
            ws = win if s == 0 else pltpu.roll(win, win_rows - s, axis=0)
            for q in range(win_rows // SUBLANES):
                j = SUBLANES * q + s - off
                if 0 <= j < CONV_WIDTH:
                    part = part + ws[SUBLANES * q:SUBLANES * q + rb, :] * w_ref[j:j + 1, l0:l0 + lb]
        acc_ref[r0:r0 + rb, l0:l0 + lb] = part
    acc = acc_ref[r0:r0 + rb, :]
    mu = jnp.mean(acc, axis=-1, keepdims=True)
    dev = acc - mu
    var = jnp.mean(dev * dev, axis=-1, keepdims=True)
    y = dev * lax.rsqrt(var + LN_EPS) * lng_ref[...] + lnb_ref[...]
    out_ref[r0:r0 + rb, :] = (y * jax.nn.sigmoid(y)).astype(out_ref.dtype)
    return y


def _back_kernel(x_ref, a_ref, u0_ref, un_ref, ga_ref, gc_ref, cw_ref, cb_ref, clg_ref, clb_ref,
                 wa_ref, wc_ref, wo_ref, nff_ref, w1_ref, w2_ref, nfin_ref, o_ref,
                 ubuf_ref, cacc_ref, ccur_ref, cnext_ref):
    d = x_ref.shape[-1]
    dff = w1_ref.shape[1]
    conv_refs = (cw_ref, cb_ref, clg_ref, clb_ref)

    tm = x_ref.shape[0]
    n_conv = tm // CONV_ROWS
    n_mlp = dff // d
    assert n_conv == n_mlp, "one conv row block is paired with each MLP chunk"

    @pl.when(pl.program_id(1) == 0)
    def _():
        ubuf_ref[0:CONV_HALO, :] = jnp.zeros((CONV_HALO, d), F32)
        ubuf_ref[CONV_HALO:, :] = u0_ref[...].astype(F32)
        for i in range(n_conv):
            _conv_rows(i * CONV_ROWS, *conv_refs, ccur_ref, ubuf_ref, cacc_ref)
        ubuf_ref[0:CONV_HALO, :] = ubuf_ref[tm:tm + CONV_HALO, :]

    ubuf_ref[CONV_HALO:, :] = un_ref[...].astype(F32)
    merged = (ga_ref[...].astype(F32) * _dot(a_ref[...], wa_ref[...])
              + gc_ref[...].astype(F32) * _dot(ccur_ref[...], wc_ref[...]))
    x1 = x_ref[...] + _dot(merged.astype(BF16), wo_ref[...])
    h2 = (x1 * lax.rsqrt(jnp.mean(x1 * x1, axis=-1, keepdims=True) + RMS_EPS) * nff_ref[...]).astype(BF16)
    acc = x1
    conv_zero = None
    for i in range(n_mlp):
        y = _conv_rows(i * CONV_ROWS, *conv_refs, cnext_ref, ubuf_ref, cacc_ref, tie=_zero_of(acc))
        h2i = h2 if conv_zero is None else h2 + conv_zero.astype(BF16)
        hid = jnp.maximum(_dot(h2i, w1_ref[:, i * d:(i + 1) * d]), 0.0)
        acc = acc + _dot((hid * hid).astype(BF16), w2_ref[i * d:(i + 1) * d, :])
        conv_zero = _zero_of(y)
    acc = acc + conv_zero
    o_ref[...] = acc * lax.rsqrt(jnp.mean(acc * acc, axis=-1, keepdims=True) + RMS_EPS) * nfin_ref[...]
    ubuf_ref[0:CONV_HALO, :] = ubuf_ref[tm:tm + CONV_HALO, :]
    ccur_ref[...] = cnext_ref[...]


def _back(x, a, u, ga, gc, conv_w, conv_b, conv_ln_g, conv_ln_b, wa, wc, wo, nff, w1, w2, nfin, *, tm):
    b, t, d = x.shape
    assert t % tm == 0 and tm % 64 == 0
    nt = t // tm
    tile = pl.BlockSpec((None, tm, d), lambda i, j: (i, j, 0))
    first_tile = pl.BlockSpec((None, tm, d), lambda i, j: (i, 0, 0))
    next_tile = pl.BlockSpec((None, tm, d), lambda i, j: (i, jnp.minimum(j + 1, nt - 1), 0))

    def full(arr, single=False):
        kw = {"pipeline_mode": pl.Buffered(1)} if single else {}
        return pl.BlockSpec(arr.shape, lambda i, j: (0,) * arr.ndim, **kw)

    return pl.pallas_call(
        _back_kernel,
        grid=(b, nt),
        in_specs=[tile, tile, first_tile, next_tile, tile, tile,
                  full(conv_w), full(conv_b), full(conv_ln_g), full(conv_ln_b),
                  full(wa, True), full(wc, True), full(wo, True), full(nff),
                  full(w1, True), full(w2, True), full(nfin)],
        out_specs=tile,
        out_shape=jax.ShapeDtypeStruct((b, t, d), F32),
        scratch_shapes=[
            pltpu.VMEM((tm + CONV_HALO, d), F32),
            pltpu.VMEM((tm, d), F32),
            pltpu.VMEM((tm, d), BF16),
            pltpu.VMEM((tm, d), BF16),
        ],
        compiler_params=pltpu.CompilerParams(
            dimension_semantics=("arbitrary", "arbitrary"), vmem_limit_bytes=VMEM_LIMIT_BYTES),
        name="back",
    )(x, a, u, u, ga, gc, conv_w, conv_b, conv_ln_g, conv_ln_b, wa, wc, wo, nff, w1, w2, nfin)


def _tile(t, want):
    while t % want:
        want //= 2
    return want


def kernel(x, norm_mix_g, w_in, b_gate, mu_rkv, mu_lora, decay_w0, decay_w1, decay_w2, aaa_a0, aaa_a1, aaa_a2, gate_g1, gate_g2, k_k, k_a, r_k, ln_x_g, ln_x_b, w_rwkv_proj, conv_w, conv_b, conv_ln_g, conv_ln_b, w_conv_proj, w_out, norm_ff_g, w_ff1, w_ff2, norm_final_g):
    assert w_in.shape[0] == 1, "single trunk layer"
    t = x.shape[1]
    bf = lambda z: z[0].astype(BF16)
    row = lambda z: z.reshape(1, -1)
    r, k, v, lw, a, g, u, ga, gc = _front(
        x, row(norm_mix_g), bf(w_in), row(b_gate), row(mu_rkv), mu_lora[0],
        row(decay_w0), bf(decay_w1), bf(decay_w2),
        row(aaa_a0), bf(aaa_a1), bf(aaa_a2), bf(gate_g1), bf(gate_g2),
        tm=_tile(t, 256))
    a_out = _wkv(r, k, v, lw, a, g, row(k_k), row(k_a), row(r_k), row(ln_x_g), row(ln_x_b),
                 tt=_tile(t, 512))
    return _back(x, a_out, u, ga, gc, conv_w[0], row(conv_b), row(conv_ln_g), row(conv_ln_b),
                 bf(w_rwkv_proj), bf(w_conv_proj), bf(w_out),
                 row(norm_ff_g), bf(w_ff1), bf(w_ff2), row(norm_final_g), tm=_tile(t, 256))
```

```python
import functools
import math

import jax
import jax.numpy as jnp
from jax import lax
from jax.experimental import pallas as pl
from jax.experimental.pallas import tpu as pltpu

HEAD_SIZE = 64
WKV_CHUNK = 64
HEADS_PER_GROUP = 4
GROUP_LANES = HEADS_PER_GROUP * HEAD_SIZE
PASS_A_CHUNKS = 4
CONV_WIDTH = 31
CONV_HALO = 32
RMS_EPS = 1e-6
LN_EPS = 1e-5
GN_EPS = 64e-5
DECAY_SCALE = math.exp(-0.5)
SUBLANES = 8
MXU_COLS = 256

VMEM_LIMIT_BYTES = 56 * 1024 * 1024

F32 = jnp.float32
BF16 = jnp.bfloat16


def _dot(a, b):
    return jnp.dot(a, b, preferred_element_type=F32)


def _dot_nt(a, b):
    return lax.dot_general(a, b, (((1,), (1,)), ((), ())), preferred_element_type=F32)


def _shift_rows(cur, carry_ref):
    rolled = pltpu.roll(cur, 1, axis=0)
    row = lax.broadcasted_iota(jnp.int32, cur.shape, 0)
    prev = jnp.where(row == 0, carry_ref[SUBLANES - 1:SUBLANES, :], rolled)
    carry_ref[...] = cur[cur.shape[0] - SUBLANES:, :]
    return prev


def _front_kernel(x_ref, ng_ref, w_in_ref, bg_ref, mu_rkv_ref, mu_lora_ref,
                  w0_ref, w1_ref, w2_ref, a0_ref, a1_ref, a2_ref, g1_ref, g2_ref,
                  r_ref, k_ref, v_ref, lw_ref, a_ref, g_ref, u_ref, ga_ref, gc_ref,
                  hcar_ref, pcar_ref):
    d = x_ref.shape[-1]

    @pl.when(pl.program_id(1) == 0)
    def _():
        hcar_ref[...] = jnp.zeros_like(hcar_ref)
        pcar_ref[...] = jnp.zeros_like(pcar_ref)

    x = x_ref[...]
    h = x * lax.rsqrt(jnp.mean(x * x, axis=-1, keepdims=True) + RMS_EPS) * ng_ref[...]
    hb = h.astype(BF16)

    dh = _shift_rows(h, hcar_ref) - h
    xw = (h + dh * mu_lora_ref[0:1, :]).astype(BF16)
    xa = (h + dh * mu_lora_ref[1:2, :]).astype(BF16)
    xg = (h + dh * mu_lora_ref[2:3, :]).astype(BF16)
    lora_w = jnp.tanh(_dot(xw, w1_ref[...])).astype(BF16)
    lora_a = _dot(xa, a1_ref[...]).astype(BF16)
    lora_g = jax.nn.sigmoid(_dot(xg, g1_ref[...])).astype(BF16)

    for c, o_ref in enumerate((r_ref, k_ref, v_ref)):
        p = _dot(hb, w_in_ref[:, c * d:(c + 1) * d])
        p_prev = _shift_rows(p, pcar_ref.at[c])
        o_ref[...] = (p + (p_prev - p) * mu_rkv_ref[:, c * d:(c + 1) * d]).astype(o_ref.dtype)

    pa = _dot(hb, w_in_ref[:, 3 * d:4 * d])
    pb = _dot(hb, w_in_ref[:, 4 * d:5 * d])
    u_ref[...] = (pa * jax.nn.sigmoid(pb)).astype(u_ref.dtype)

    for c, o_ref in enumerate((ga_ref, gc_ref)):
        p = _dot(hb, w_in_ref[:, (5 + c) * d:(6 + c) * d]) + bg_ref[:, c * d:(c + 1) * d]
        o_ref[...] = jax.nn.sigmoid(p).astype(o_ref.dtype)

    zw = w0_ref[...] + _dot(lora_w, w2_ref[...])
    lw_ref[...] = -DECAY_SCALE * jax.nn.sigmoid(zw)
    za = a0_ref[...] + _dot(lora_a, a2_ref[...])
    a_ref[...] = jax.nn.sigmoid(za).astype(a_ref.dtype)
    g_ref[...] = _dot(lora_g, g2_ref[...]).astype(g_ref.dtype)


def _front(x, ng, w_in, bg, mu_rkv, mu_lora, w0, w1, w2, a0, a1, a2, g1, g2, *, tm):
    b, t, d = x.shape
    assert t % tm == 0
    tile = pl.BlockSpec((None, tm, d), lambda i, j: (i, j, 0))

    def full(arr, single=False):
        kw = {"pipeline_mode": pl.Buffered(1)} if single else {}
        return pl.BlockSpec(arr.shape, lambda i, j: (0,) * arr.ndim, **kw)

    ins = (x, ng, w_in, bg, mu_rkv, mu_lora, w0, w1, w2, a0, a1, a2, g1, g2)
    in_specs = [tile, full(ng), full(w_in, single=True)] + [full(z) for z in ins[3:]]
    act = jax.ShapeDtypeStruct((b, t, d), BF16)
    out_shape = (act, act, act, jax.ShapeDtypeStruct((b, t, d), F32), act, act, act, act, act)
    return pl.pallas_call(
        _front_kernel,
        grid=(b, t // tm),
        in_specs=in_specs,
        out_specs=[tile] * len(out_shape),
        out_shape=out_shape,
        scratch_shapes=[pltpu.VMEM((SUBLANES, d), F32), pltpu.VMEM((3, SUBLANES, d), F32)],
        compiler_params=pltpu.CompilerParams(
            dimension_semantics=("arbitrary", "arbitrary"), vmem_limit_bytes=VMEM_LIMIT_BYTES),
        name="front",
    )(*ins)


def _cumsum_rows(x):
    n = x.shape[0]
    row = lax.broadcasted_iota(jnp.int32, x.shape, 0) % SUBLANES
    s = 1
    while s < SUBLANES:
        x = x + jnp.where(row >= s, pltpu.roll(x, s, axis=0), 0.0)
        s *= 2
    blocks = [x[0:SUBLANES]]
    for i in range(1, n // SUBLANES):
        blocks.append(x[i * SUBLANES:(i + 1) * SUBLANES] + blocks[-1][SUBLANES - 1:SUBLANES, :])
    return jnp.concatenate(blocks, axis=0)


LANES = 128
HEADS_PER_REG = LANES // HEAD_SIZE


def _bd_rows(blocks):
    lane = lax.broadcasted_iota(jnp.int32, (1, LANES), 1) // HEAD_SIZE
    zero = jnp.zeros_like(blocks[0])
    rows = []
    for hh, blk in enumerate(blocks):
        keep = (lane == hh % HEADS_PER_REG).astype(blk.dtype)
        cols = [zero] * (GROUP_LANES // LANES)
        cols[hh // HEADS_PER_REG] = blk * keep
        rows.append(jnp.concatenate(cols, axis=1))
    return jnp.concatenate(rows, axis=0)


def _bd(x):
    return _bd_rows([x[:, (hh // HEADS_PER_REG) * LANES:(hh // HEADS_PER_REG + 1) * LANES]
                     for hh in range(HEADS_PER_GROUP)])


def _bd_t(x):
    xt = jnp.concatenate([x] * HEADS_PER_REG, axis=0).T.astype(BF16)
    return _bd_rows([xt[hh * HEAD_SIZE:(hh + 1) * HEAD_SIZE, :] for hh in range(HEADS_PER_GROUP)])


def _wkv_kernel(r_ref, k_ref, v_ref, lw_ref, a_ref, g_ref,
                kk_ref, ka_ref, rk_ref, lng_ref, lnb_ref, bdm_ref,
                o_ref, z_ref, qa_ref, oin_ref, uh_ref, bkt_ref, lc_ref):
    c = WKV_CHUNK
    gl = GROUP_LANES
    n_groups = r_ref.shape[1] // gl
    n_chunks = r_ref.shape[0] // c

    @pl.when(pl.program_id(1) == 0)
    def _():
        z_ref[...] = jnp.zeros_like(z_ref)

    row = lax.broadcasted_iota(jnp.int32, (c, gl), 0)
    colj = lax.broadcasted_iota(jnp.int32, (c, gl), 1) % HEAD_SIZE
    strict = row > colj
    incl = row >= colj
    eye = (row == colj).astype(F32)
    bdm = bdm_ref[...]
    brow = lax.broadcasted_iota(jnp.int32, (gl, gl), 0) // HEAD_SIZE
    bcol = lax.broadcasted_iota(jnp.int32, (gl, gl), 1) // HEAD_SIZE
    bd_sel = brow == bcol

    cpi = PASS_A_CHUNKS

    def pass_a(chunk_ids):
        st = []
        chains = []
        for ci in chunk_ids:
            rows = slice(ci * c, (ci + 1) * c)
            for gi in range(n_groups):
                lanes = slice(gi * gl, (gi + 1) * gl)
                k = k_ref[rows, lanes].astype(F32)
                chains.append((ci, rows, lanes, k, k * kk_ref[:, lanes]))
        sq_sums = _dot(jnp.concatenate([(kk * kk).astype(BF16) for *_, kk in chains], axis=0), bdm)
        yield
        for idx, (ci, rows, lanes, k, kk) in enumerate(chains):
            r = r_ref[rows, lanes].astype(F32)
            a_lr = a_ref[rows, lanes].astype(F32)
            lw = lw_ref[rows, lanes]
            kk = kk / jnp.maximum(jnp.sqrt(sq_sums[idx * c:(idx + 1) * c]), 1e-12)
            b = kk * a_lr
            kmod = k * (1.0 + (a_lr - 1.0) * ka_ref[:, lanes])
            cum = _cumsum_rows(lw)
            cum_last = cum[c - 1:c, :]
            e_neg = jnp.exp(-cum)
            e_rem = jnp.exp(cum_last - cum)
            at = (-kk * jnp.exp(cum - lw)).astype(BF16)
            rt = (r * jnp.exp(cum)).astype(BF16)
            bkt_ref[ci, lanes, :] = jnp.concatenate([b * e_rem, kmod * e_rem], axis=0).T.astype(BF16)
            lc_ref[ci, :, lanes] = jnp.broadcast_to(cum_last, (SUBLANES, gl))
            st.append(dict(ci=ci, lanes=lanes, at=at, rt=rt, lhs=jnp.concatenate([at, rt], axis=0),
                           bt=_bd_t(b * e_neg), kt=_bd_t(kmod * e_neg),
                           vbd=_bd(v_ref[rows, lanes])))
        for s in st:
            s["sb"] = _dot(s["lhs"], s["bt"])
        yield
        for s in st:
            s["sk"] = _dot(s["lhs"], s["kt"])
        yield
        for s in st:
            p = jnp.where(strict, s["sb"][:c], 0.0)
            s["tm"] = eye + p
            pb = p.astype(BF16)
            s["p"] = _dot(pb, _bd(pb))
        yield
        for _ in range(4):
            for s in st:
                pb = s["p"].astype(BF16)
                res = _dot(jnp.concatenate([s["tm"].astype(BF16), pb], axis=0), _bd(pb))
                s["tm"] = s["tm"] + res[:c]
                s["p"] = res[c:]
            yield
        for s in st:
            tm = s["tm"] + _dot(s["tm"].astype(BF16), _bd(s["p"].astype(BF16)))
            s["tmb"] = tm.astype(BF16)
        yield
        for s in st:
            ak = jnp.where(strict, s["sk"][:c], 0.0).astype(BF16)
            rk = jnp.where(incl, s["sk"][c:], 0.0).astype(BF16)
            res = _dot(jnp.concatenate([ak, rk], axis=0), s["vbd"])
            s["akv"] = res[:c]
            s["rkv"] = res[c:]
        yield
        for s in st:
            s["a_hat"] = _dot(s["tmb"], _bd(s["at"])).astype(BF16)
        yield
        for s in st:
            s["u_hat"] = _dot(s["tmb"], _bd(s["akv"].astype(BF16)))
        yield
        for s in st:
            s["rb"] = jnp.where(incl, s["sb"][c:], 0.0).astype(BF16)
            s["q_hat"] = s["rt"].astype(F32) + _dot(s["rb"], _bd(s["a_hat"]))
        yield
        for s in st:
            s["o_in"] = _dot(s["rb"], _bd(s["u_hat"].astype(BF16))) + s["rkv"]
        for s in st:
            ci, lanes = s["ci"], s["lanes"]
            qa_ref[ci, 0:c, lanes] = s["q_hat"].astype(BF16)
            qa_ref[ci, c:2 * c, lanes] = s["a_hat"]
            oin_ref[ci, :, lanes] = s["o_in"]
            uh_ref[ci, :, lanes] = s["u_hat"]
        yield

    def pass_b(chunk_ids):
        groups = [slice(gi * gl, (gi + 1) * gl) for gi in range(n_groups)]
        for ci in chunk_ids:
            rows = slice(ci * c, (ci + 1) * c)
            zs = [z_ref[gi] for gi in range(n_groups)]
            res = [_dot(qa_ref[ci, :, lanes], z.astype(BF16)) for lanes, z in zip(groups, zs)]
            yield
            outs = [rs[:c] + oin_ref[ci, :, lanes] for lanes, rs in zip(groups, res)]
            upd = [_dot(bkt_ref[ci, lanes, :],
                        jnp.concatenate([(rs[c:] + uh_ref[ci, :, lanes]).astype(BF16), v_ref[rows, lanes]], axis=0))
                   for lanes, rs in zip(groups, res)]
            for gi, lanes in enumerate(groups):
                w_col = jnp.exp(jnp.broadcast_to(lc_ref[ci, 0:1, lanes], (2 * c, gl)).T)
                z_ref[gi] = jnp.concatenate([w_col, w_col], axis=1) * zs[gi] + jnp.where(bd_sel, upd[gi], 0.0)
            yield
            rkr = []
            for lanes in groups:
                k = k_ref[rows, lanes].astype(F32)
                kmod = k * (1.0 + (a_ref[rows, lanes].astype(F32) - 1.0) * ka_ref[:, lanes])
                rkr.append(r_ref[rows, lanes].astype(F32) * kmod * rk_ref[:, lanes])
            sums = _dot(jnp.concatenate([x.astype(BF16) for x in outs + rkr], axis=0), bdm)
            yield
            devs = [o - sums[gi * c:(gi + 1) * c] * (1.0 / HEAD_SIZE) for gi, o in enumerate(outs)]
            sq_sums = _dot(jnp.concatenate([(dv * dv).astype(BF16) for dv in devs], axis=0), bdm)
            for gi, lanes in enumerate(groups):
                var = sq_sums[gi * c:(gi + 1) * c] * (1.0 / HEAD_SIZE)
                y = devs[gi] * lax.rsqrt(var + GN_EPS) * lng_ref[:, lanes] + lnb_ref[:, lanes]
                y = y + sums[(n_groups + gi) * c:(n_groups + gi + 1) * c] * v_ref[rows, lanes].astype(F32)
                o_ref[rows, lanes] = (y * g_ref[rows, lanes].astype(F32)).astype(o_ref.dtype)
            yield

    def weave(*gens):
        live = list(gens)
        while live:
            for gen in list(live):
                if next(gen, StopIteration) is StopIteration:
                    live.remove(gen)

    blocks = [range(i, i + cpi) for i in range(0, n_chunks, cpi)]
    weave(pass_a(blocks[0]))
    for prev, cur in zip(blocks[:-1], blocks[1:]):
        weave(pass_a(cur), pass_b(prev))
    weave(pass_b(blocks[-1]))


def _wkv(r, k, v, lw, a, g, k_k, k_a, r_k, ln_g, ln_b, *, tt):
    b, t, d = r.shape
    gl = GROUP_LANES
    c = WKV_CHUNK
    assert t % tt == 0 and tt % c == 0 and d % gl == 0
    n_chunks = tt // c
    head_of = jnp.arange(gl) // HEAD_SIZE
    bd_mask = (head_of[:, None] == head_of[None, :]).astype(BF16)
    tile = pl.BlockSpec((None, tt, d), lambda i, j: (i, j, 0))
    vec = pl.BlockSpec((1, d), lambda i, j: (0, 0))
    return pl.pallas_call(
        _wkv_kernel,
        grid=(b, t // tt),
        in_specs=[tile] * 6 + [vec] * 5 + [pl.BlockSpec((gl, gl), lambda i, j: (0, 0))],
        out_specs=tile,
        out_shape=jax.ShapeDtypeStruct((b, t, d), BF16),
        scratch_shapes=[
            pltpu.VMEM((d // gl, gl, gl), F32),
            pltpu.VMEM((n_chunks, 2 * c, d), BF16),
            pltpu.VMEM((n_chunks, c, d), F32),
            pltpu.VMEM((n_chunks, c, d), F32),
            pltpu.VMEM((n_chunks, d, 2 * c), BF16),
            pltpu.VMEM((n_chunks, SUBLANES, d), F32),
        ],
        compiler_params=pltpu.CompilerParams(
            dimension_semantics=("arbitrary", "arbitrary"), vmem_limit_bytes=VMEM_LIMIT_BYTES),
        name="wkv",
    )(r, k, v, lw, a, g, k_k, k_a, r_k, ln_g, ln_b, bd_mask)


CONV_ROWS = 64
CONV_LANES = 128


def _zero_of(v):
    bits = pltpu.bitcast(v[0:SUBLANES, :].astype(F32), jnp.uint32)
    sixteen = jnp.uint32(16)
    bits = lax.shift_right_logical(lax.shift_right_logical(bits, sixteen), sixteen)
    return pltpu.bitcast(bits, F32)[0:1, :]


def _conv_acc(r0, l_lo, l_hi, w_ref, cb_ref, buf_ref, acc_ref, tie=None):
    rb, lb = CONV_ROWS, CONV_LANES
    off = CONV_HALO - (CONV_WIDTH - 1)
    win_rows = rb + CONV_HALO
    bias = cb_ref[...] if tie is None else cb_ref[...] + tie
    zeros = []
    for l0 in range(l_lo, l_hi, lb):
        win = buf_ref[r0:r0 + win_rows, l0:l0 + lb]
        part = jnp.zeros((rb, lb), F32) + bias[:, l0:l0 + lb]
        for s in range(SUBLANES):
            ws = win if s == 0 else pltpu.roll(win, win_rows - s, axis=0)
            for q in range(win_rows // SUBLANES):
                j = SUBLANES * q + s - off
                if 0 <= j < CONV_WIDTH:
                    part = part + ws[SUBLANES * q:SUBLANES * q + rb, :] * w_ref[j:j + 1, l0:l0 + lb]
        acc_ref[r0:r0 + rb, l0:l0 + lb] = part
        zeros.append(_zero_of(part))
    return jnp.concatenate(zeros, axis=1)


def _conv_norm(r0, lng_ref, lnb_ref, out_ref, acc_ref):
    acc = acc_ref[r0:r0 + CONV_ROWS, :]
    mu = jnp.mean(acc, axis=-1, keepdims=True)
    dev = acc - mu
    var = jnp.mean(dev * dev, axis=-1, keepdims=True)
    y = dev * lax.rsqrt(var + LN_EPS) * lng_ref[...] + lnb_ref[...]
    out_ref[r0:r0 + CONV_ROWS, :] = (y * jax.nn.sigmoid(y)).astype(out_ref.dtype)
    return y


def _back_kernel(x_ref, a_ref, u0_ref, un_ref, ga_ref, gc_ref, cw_ref, cb_ref, clg_ref, clb_ref,
                 wa_ref, wc_ref, wo_ref, nff_ref, w1_ref, w2_ref, nfin_ref, o_ref,
                 ubuf_ref, cacc_ref, ccur_ref, cnext_ref):
    d = x_ref.shape[-1]
    dff = w1_ref.shape[1]
    tm = x_ref.shape[0]
    n_conv = tm // CONV_ROWS
    n_mlp = dff // d
    half = d // 2
    assert n_conv == n_mlp, "one conv row block is paired with each MLP chunk"

    @pl.when(pl.program_id(1) == 0)
    def _():
        ubuf_ref[0:CONV_HALO, :] = jnp.zeros((CONV_HALO, d), F32)
        ubuf_ref[CONV_HALO:, :] = u0_ref[...].astype(F32)
        for i in range(n_conv):
            _conv_acc(i * CONV_ROWS, 0, d, cw_ref, cb_ref, ubuf_ref, cacc_ref)
            _conv_norm(i * CONV_ROWS, clg_ref, clb_ref, ccur_ref, cacc_ref)
        ubuf_ref[0:CONV_HALO, :] = ubuf_ref[tm:tm + CONV_HALO, :]

    ubuf_ref[CONV_HALO:, :] = un_ref[...].astype(F32)
    merged = (ga_ref[...].astype(F32) * _dot(a_ref[...], wa_ref[...])
              + gc_ref[...].astype(F32) * _dot(ccur_ref[...], wc_ref[...]))
    x1 = x_ref[...] + _dot(merged.astype(BF16), wo_ref[...])
    h2 = (x1 * lax.rsqrt(jnp.mean(x1 * x1, axis=-1, keepdims=True) + RMS_EPS) * nff_ref[...]).astype(BF16)
    acc = x1
    conv_zero = None
    for i in range(n_mlp):
        r0 = i * CONV_ROWS
        zero_a = _conv_acc(r0, 0, half, cw_ref, cb_ref, ubuf_ref, cacc_ref, tie=_zero_of(acc))
        h2i = h2 if conv_zero is None else h2 + conv_zero.astype(BF16)
        hid = jnp.maximum(_dot(h2i, w1_ref[:, i * d:(i + 1) * d]), 0.0)
        _conv_acc(r0, half, d, cw_ref, cb_ref, ubuf_ref, cacc_ref, tie=_zero_of(hid))
        conv_zero = _zero_of(_conv_norm(r0, clg_ref, clb_ref, cnext_ref, cacc_ref))
        hid2 = (hid * hid).astype(BF16) + jnp.concatenate([zero_a, zero_a], axis=1).astype(BF16)
        acc = acc + _dot(hid2, w2_ref[i * d:(i + 1) * d, :])
    acc = acc + conv_zero
    o_ref[...] = acc * lax.rsqrt(jnp.mean(acc * acc, axis=-1, keepdims=True) + RMS_EPS) * nfin_ref[...]
    ubuf_ref[0:CONV_HALO, :] = ubuf_ref[tm:tm + CONV_HALO, :]
    ccur_ref[...] = cnext_ref[...]


def _back(x, a, u, ga, gc, conv_w, conv_b, conv_ln_g, conv_ln_b, wa, wc, wo, nff, w1, w2, nfin, *, tm):
    b, t, d = x.shape
    assert t % tm == 0 and tm % 64 == 0
    nt = t // tm
    tile = pl.BlockSpec((None, tm, d), lambda i, j: (i, j, 0))
    first_tile = pl.BlockSpec((None, tm, d), lambda i, j: (i, 0, 0))
    next_tile = pl.BlockSpec((None, tm, d), lambda i, j: (i, jnp.minimum(j + 1, nt - 1), 0))

    def full(arr, single=False):
        kw = {"pipeline_mode": pl.Buffered(1)} if single else {}
        return pl.BlockSpec(arr.shape, lambda i, j: (0,) * arr.ndim, **kw)

    return pl.pallas_call(
        _back_kernel,
        grid=(b, nt),
        in_specs=[tile, tile, first_tile, next_tile, tile, tile,
                  full(conv_w), full(conv_b), full(conv_ln_g), full(conv_ln_b),
                  full(wa, True), full(wc, True), full(wo, True), full(nff),
                  full(w1, True), full(w2, True), full(nfin)],
        out_specs=tile,
        out_shape=jax.ShapeDtypeStruct((b, t, d), F32),
        scratch_shapes=[
            pltpu.VMEM((tm + CONV_HALO, d), F32),
            pltpu.VMEM((tm, d), F32),
            pltpu.VMEM((tm, d), BF16),
            pltpu.VMEM((tm, d), BF16),
        ],
        compiler_params=pltpu.CompilerParams(
            dimension_semantics=("arbitrary", "arbitrary"), vmem_limit_bytes=VMEM_LIMIT_BYTES),
        name="back",
    )(x, a, u, u, ga, gc, conv_w, conv_b, conv_ln_g, conv_ln_b, wa, wc, wo, nff, w1, w2, nfin)


def _tile(t, want):
    while t % want:
        want //= 2
    return want


def kernel(x, norm_mix_g, w_in, b_gate, mu_rkv, mu_lora, decay_w0, decay_w1, decay_w2, aaa_a0, aaa_a1, aaa_a2, gate_g1, gate_g2, k_k, k_a, r_k, ln_x_g, ln_x_b, w_rwkv_proj, conv_w, conv_b, conv_ln_g, conv_ln_b, w_conv_proj, w_out, norm_ff_g, w_ff1, w_ff2, norm_final_g):
    assert w_in.shape[0] == 1, "single trunk layer"
    t = x.shape[1]
    bf = lambda z: z[0].astype(BF16)
    row = lambda z: z.reshape(1, -1)
    r, k, v, lw, a, g, u, ga, gc = _front(
        x, row(norm_mix_g), bf(w_in), row(b_gate), row(mu_rkv), mu_lora[0],
        row(decay_w0), bf(decay_w1), bf(decay_w2),
        row(aaa_a0), bf(aaa_a1), bf(aaa_a2), bf(gate_g1), bf(gate_g2),
        tm=_tile(t, 256))
    a_out = _wkv(r, k, v, lw, a, g, row(k_k), row(k_a), row(r_k), row(ln_x_g), row(ln_x_b),
                 tt=_tile(t, 512))
    return _back(x, a_out, u, ga, gc, conv_w[0], row(conv_b), row(conv_ln_g), row(conv_ln_b),
                 bf(w_rwkv_proj), bf(w_conv_proj), bf(w_out),
                 row(norm_ff_g), bf(w_ff1), bf(w_ff2), row(norm_final_g), tm=_tile(t, 256))
```

```python
import functools
import math

import jax
import jax.numpy as jnp
from jax import lax
from jax.experimental import pallas as pl
from jax.experimental.pallas import tpu as pltpu

HEAD_SIZE = 64
WKV_CHUNK = 64
HEADS_PER_GROUP = 4
GROUP_LANES = HEADS_PER_GROUP * HEAD_SIZE
WKV_BLOCK_CHUNKS = 4
CONV_WIDTH = 31
CONV_HALO = 32
RMS_EPS = 1e-6
LN_EPS = 1e-5
GN_EPS = 64e-5
DECAY_SCALE = math.exp(-0.5)
SUBLANES = 8
MXU_COLS = 256

VMEM_LIMIT_BYTES = 56 * 1024 * 1024

F32 = jnp.float32
BF16 = jnp.bfloat16


def _dot(a, b):
    return jnp.dot(a, b, preferred_element_type=F32)


def _dot_nt(a, b):
    return lax.dot_general(a, b, (((1,), (1,)), ((), ())), preferred_element_type=F32)


def _shift_rows(cur, carry_ref):
    rolled = pltpu.roll(cur, 1, axis=0)
    row = lax.broadcasted_iota(jnp.int32, cur.shape, 0)
    prev = jnp.where(row == 0, carry_ref[SUBLANES - 1:SUBLANES, :], rolled)
    carry_ref[...] = cur[cur.shape[0] - SUBLANES:, :]
    return prev


def _front_kernel(x_ref, ng_ref, w_in_ref, bg_ref, mu_rkv_ref, mu_lora_ref,
                  w0_ref, w1_ref, w2_ref, a0_ref, a1_ref, a2_ref, g1_ref, g2_ref,
                  r_ref, k_ref, v_ref, lw_ref, a_ref, g_ref, u_ref, ga_ref, gc_ref,
                  hcar_ref, pcar_ref):
    d = x_ref.shape[-1]

    @pl.when(pl.program_id(1) == 0)
    def _():
        hcar_ref[...] = jnp.zeros_like(hcar_ref)
        pcar_ref[...] = jnp.zeros_like(pcar_ref)

    x = x_ref[...]
    h = x * lax.rsqrt(jnp.mean(x * x, axis=-1, keepdims=True) + RMS_EPS) * ng_ref[...]
    hb = h.astype(BF16)

    dh = _shift_rows(h, hcar_ref) - h
    xw = (h + dh * mu_lora_ref[0:1, :]).astype(BF16)
    xa = (h + dh * mu_lora_ref[1:2, :]).astype(BF16)
    xg = (h + dh * mu_lora_ref[2:3, :]).astype(BF16)
    lora_w = jnp.tanh(_dot(xw, w1_ref[...])).astype(BF16)
    lora_a = _dot(xa, a1_ref[...]).astype(BF16)
    lora_g = jax.nn.sigmoid(_dot(xg, g1_ref[...])).astype(BF16)

    for c, o_ref in enumerate((r_ref, k_ref, v_ref)):
        p = _dot(hb, w_in_ref[:, c * d:(c + 1) * d])
        p_prev = _shift_rows(p, pcar_ref.at[c])
        o_ref[...] = (p + (p_prev - p) * mu_rkv_ref[:, c * d:(c + 1) * d]).astype(o_ref.dtype)

    pa = _dot(hb, w_in_ref[:, 3 * d:4 * d])
    pb = _dot(hb, w_in_ref[:, 4 * d:5 * d])
    u_ref[...] = (pa * jax.nn.sigmoid(pb)).astype(u_ref.dtype)

    for c, o_ref in enumerate((ga_ref, gc_ref)):
        p = _dot(hb, w_in_ref[:, (5 + c) * d:(6 + c) * d]) + bg_ref[:, c * d:(c + 1) * d]
        o_ref[...] = jax.nn.sigmoid(p).astype(o_ref.dtype)

    zw = w0_ref[...] + _dot(lora_w, w2_ref[...])
    lw_ref[...] = -DECAY_SCALE * jax.nn.sigmoid(zw)
    za = a0_ref[...] + _dot(lora_a, a2_ref[...])
    a_ref[...] = jax.nn.sigmoid(za).astype(a_ref.dtype)
    g_ref[...] = _dot(lora_g, g2_ref[...]).astype(g_ref.dtype)


def _front(x, ng, w_in, bg, mu_rkv, mu_lora, w0, w1, w2, a0, a1, a2, g1, g2, *, tm):
    b, t, d = x.shape
    assert t % tm == 0
    tile = pl.BlockSpec((None, tm, d), lambda i, j: (i, j, 0))

    def full(arr, single=False):
        kw = {"pipeline_mode": pl.Buffered(1)} if single else {}
        return pl.BlockSpec(arr.shape, lambda i, j: (0,) * arr.ndim, **kw)

    ins = (x, ng, w_in, bg, mu_rkv, mu_lora, w0, w1, w2, a0, a1, a2, g1, g2)
    in_specs = [tile, full(ng), full(w_in, single=True)] + [full(z) for z in ins[3:]]
    act = jax.ShapeDtypeStruct((b, t, d), BF16)
    out_shape = (act, act, act, jax.ShapeDtypeStruct((b, t, d), F32), act, act, act, act, act)
    return pl.pallas_call(
        _front_kernel,
        grid=(b, t // tm),
        in_specs=in_specs,
        out_specs=[tile] * len(out_shape),
        out_shape=out_shape,
        scratch_shapes=[pltpu.VMEM((SUBLANES, d), F32), pltpu.VMEM((3, SUBLANES, d), F32)],
        compiler_params=pltpu.CompilerParams(
            dimension_semantics=("arbitrary", "arbitrary"), vmem_limit_bytes=VMEM_LIMIT_BYTES),
        name="front",
    )(*ins)


def _cumsum_rows(x):
    n = x.shape[0]
    row = lax.broadcasted_iota(jnp.int32, x.shape, 0) % SUBLANES
    s = 1
    while s < SUBLANES:
        x = x + jnp.where(row >= s, pltpu.roll(x, s, axis=0), 0.0)
        s *= 2
    blocks = [x[0:SUBLANES]]
    for i in range(1, n // SUBLANES):
        blocks.append(x[i * SUBLANES:(i + 1) * SUBLANES] + blocks[-1][SUBLANES - 1:SUBLANES, :])
    return jnp.concatenate(blocks, axis=0)


LANES = 128
HEADS_PER_REG = LANES // HEAD_SIZE


def _bd_rows(blocks):
    lane = lax.broadcasted_iota(jnp.int32, (1, LANES), 1) // HEAD_SIZE
    zero = jnp.zeros_like(blocks[0])
    rows = []
    for hh, blk in enumerate(blocks):
        keep = (lane == hh % HEADS_PER_REG).astype(blk.dtype)
        cols = [zero] * (GROUP_LANES // LANES)
        cols[hh // HEADS_PER_REG] = blk * keep
        rows.append(jnp.concatenate(cols, axis=1))
    return jnp.concatenate(rows, axis=0)


def _bd(x):
    return _bd_rows([x[:, (hh // HEADS_PER_REG) * LANES:(hh // HEADS_PER_REG + 1) * LANES]
                     for hh in range(HEADS_PER_GROUP)])


def _bd_t(x):
    xt = jnp.concatenate([x] * HEADS_PER_REG, axis=0).T.astype(BF16)
    return _bd_rows([xt[hh * HEAD_SIZE:(hh + 1) * HEAD_SIZE, :] for hh in range(HEADS_PER_GROUP)])


def _wkv_kernel(r_ref, k_ref, v_ref, lw_ref, a_ref, g_ref,
                kk_ref, ka_ref, rk_ref, lng_ref, lnb_ref, bdm_ref,
                o_ref, z_ref, qa_ref, oin_ref, uh_ref, bkt_ref, lc_ref):
    c = WKV_CHUNK
    gl = GROUP_LANES
    n_groups = r_ref.shape[1] // gl
    n_chunks = r_ref.shape[0] // c

    @pl.when(pl.program_id(1) == 0)
    def _():
        z_ref[...] = jnp.zeros_like(z_ref)

    row = lax.broadcasted_iota(jnp.int32, (c, gl), 0)
    colj = lax.broadcasted_iota(jnp.int32, (c, gl), 1) % HEAD_SIZE
    strict = row > colj
    incl = row >= colj
    eye = (row == colj).astype(F32)
    bdm = bdm_ref[...]
    brow = lax.broadcasted_iota(jnp.int32, (gl, gl), 0) // HEAD_SIZE
    bcol = lax.broadcasted_iota(jnp.int32, (gl, gl), 1) // HEAD_SIZE
    bd_sel = brow == bcol

    def pass_a(chunk_ids):
        st = []
        chains = []
        for ci in chunk_ids:
            rows = slice(ci * c, (ci + 1) * c)
            for gi in range(n_groups):
                lanes = slice(gi * gl, (gi + 1) * gl)
                k = k_ref[rows, lanes].astype(F32)
                chains.append((ci, rows, lanes, k, k * kk_ref[:, lanes]))
        sq_sums = _dot(jnp.concatenate([(kk * kk).astype(BF16) for *_, kk in chains], axis=0), bdm)
        yield
        for idx, (ci, rows, lanes, k, kk) in enumerate(chains):
            r = r_ref[rows, lanes].astype(F32)
            a_lr = a_ref[rows, lanes].astype(F32)
            lw = lw_ref[rows, lanes]
            kk = kk / jnp.maximum(jnp.sqrt(sq_sums[idx * c:(idx + 1) * c]), 1e-12)
            b = kk * a_lr
            kmod = k * (1.0 + (a_lr - 1.0) * ka_ref[:, lanes])
            cum = _cumsum_rows(lw)
            cum_last = cum[c - 1:c, :]
            e_neg = jnp.exp(-cum)
            e_rem = jnp.exp(cum_last - cum)
            at = (-kk * jnp.exp(cum - lw)).astype(BF16)
            rt = (r * jnp.exp(cum)).astype(BF16)
            bkt_ref[ci, lanes, :] = jnp.concatenate([b * e_rem, kmod * e_rem], axis=0).T.astype(BF16)
            lc_ref[ci, :, lanes] = jnp.broadcast_to(cum_last, (SUBLANES, gl))
            st.append(dict(ci=ci, lanes=lanes, at=at, rt=rt, lhs=jnp.concatenate([at, rt], axis=0),
                           bt=_bd_t(b * e_neg), kt=_bd_t(kmod * e_neg),
                           vbd=_bd(v_ref[rows, lanes])))
        for s in st:
            s["sb"] = _dot(s["lhs"], s["bt"])
        yield
        for s in st:
            s["sk"] = _dot(s["lhs"], s["kt"])
        yield
        for s in st:
            p = jnp.where(strict, s["sb"][:c], 0.0)
            s["tm"] = eye + p
            pb = p.astype(BF16)
            s["p"] = _dot(pb, _bd(pb))
        yield
        for _ in range(4):
            for s in st:
                pb = s["p"].astype(BF16)
                res = _dot(jnp.concatenate([s["tm"].astype(BF16), pb], axis=0), _bd(pb))
                s["tm"] = s["tm"] + res[:c]
                s["p"] = res[c:]
            yield
        for s in st:
            tm = s["tm"] + _dot(s["tm"].astype(BF16), _bd(s["p"].astype(BF16)))
            s["tmb"] = tm.astype(BF16)
        yield
        for s in st:
            ak = jnp.where(strict, s["sk"][:c], 0.0).astype(BF16)
            rk = jnp.where(incl, s["sk"][c:], 0.0).astype(BF16)
            res = _dot(jnp.concatenate([ak, rk], axis=0), s["vbd"])
            s["akv"] = res[:c]
            s["rkv"] = res[c:]
        yield
        for s in st:
            s["a_hat"] = _dot(s["tmb"], _bd(s["at"])).astype(BF16)
        yield
        for s in st:
            s["u_hat"] = _dot(s["tmb"], _bd(s["akv"].astype(BF16)))
        yield
        for s in st:
            s["rb"] = jnp.where(incl, s["sb"][c:], 0.0).astype(BF16)
            s["q_hat"] = s["rt"].astype(F32) + _dot(s["rb"], _bd(s["a_hat"]))
        yield
        for s in st:
            s["o_in"] = _dot(s["rb"], _bd(s["u_hat"].astype(BF16))) + s["rkv"]
        for s in st:
            ci, lanes = s["ci"], s["lanes"]
            qa_ref[ci, 0:c, lanes] = s["q_hat"].astype(BF16)
            qa_ref[ci, c:2 * c, lanes] = s["a_hat"]
            oin_ref[ci, :, lanes] = s["o_in"]
            uh_ref[ci, :, lanes] = s["u_hat"]
        yield

    def pass_b(chunk_ids):
        groups = [slice(gi * gl, (gi + 1) * gl) for gi in range(n_groups)]
        for ci in chunk_ids:
            rows = slice(ci * c, (ci + 1) * c)
            zs = [z_ref[gi] for gi in range(n_groups)]
            res = [_dot(qa_ref[ci, :, lanes], z.astype(BF16)) for lanes, z in zip(groups, zs)]
            yield
            outs = [rs[:c] + oin_ref[ci, :, lanes] for lanes, rs in zip(groups, res)]
            upd = [_dot(bkt_ref[ci, lanes, :],
                        jnp.concatenate([(rs[c:] + uh_ref[ci, :, lanes]).astype(BF16), v_ref[rows, lanes]], axis=0))
                   for lanes, rs in zip(groups, res)]
            for gi, lanes in enumerate(groups):
                w_col = jnp.exp(jnp.broadcast_to(lc_ref[ci, 0:1, lanes], (2 * c, gl)).T)
                z_ref[gi] = jnp.concatenate([w_col, w_col], axis=1) * zs[gi] + jnp.where(bd_sel, upd[gi], 0.0)
            yield
            rkr = []
            for lanes in groups:
                k = k_ref[rows, lanes].astype(F32)
                kmod = k * (1.0 + (a_ref[rows, lanes].astype(F32) - 1.0) * ka_ref[:, lanes])
                rkr.append(r_ref[rows, lanes].astype(F32) * kmod * rk_ref[:, lanes])
            sums = _dot(jnp.concatenate([x.astype(BF16) for x in outs + rkr], axis=0), bdm)
            yield
            devs = [o - sums[gi * c:(gi + 1) * c] * (1.0 / HEAD_SIZE) for gi, o in enumerate(outs)]
            sq_sums = _dot(jnp.concatenate([(dv * dv).astype(BF16) for dv in devs], axis=0), bdm)
            for gi, lanes in enumerate(groups):
                var = sq_sums[gi * c:(gi + 1) * c] * (1.0 / HEAD_SIZE)
                y = devs[gi] * lax.rsqrt(var + GN_EPS) * lng_ref[:, lanes] + lnb_ref[:, lanes]
                y = y + sums[(n_groups + gi) * c:(n_groups + gi + 1) * c] * v_ref[rows, lanes].astype(F32)
                o_ref[rows, lanes] = (y * g_ref[rows, lanes].astype(F32)).astype(o_ref.dtype)
            yield

    def weave(*gens):
        live = list(gens)
        while live:
            for gen in list(live):
                if next(gen, StopIteration) is StopIteration:
                    live.remove(gen)

    cpi = WKV_BLOCK_CHUNKS
    blocks = [range(i, i + cpi) for i in range(0, n_chunks, cpi)]
    weave(pass_a(blocks[0]))
    for prev, cur in zip(blocks[:-1], blocks[1:]):
        weave(pass_a(cur), pass_b(prev))
    weave(pass_b(blocks[-1]))


def _wkv(r, k, v, lw, a, g, k_k, k_a, r_k, ln_g, ln_b, *, tt):
    b, t, d = r.shape
    gl = GROUP_LANES
    c = WKV_CHUNK
    assert t % tt == 0 and tt % c == 0 and d % gl == 0
    n_chunks = tt // c
    assert n_chunks % WKV_BLOCK_CHUNKS == 0
    head_of = jnp.arange(gl) // HEAD_SIZE
    bd_mask = (head_of[:, None] == head_of[None, :]).astype(BF16)
    tile = pl.BlockSpec((None, tt, d), lambda i, j: (i, j, 0))
    vec = pl.BlockSpec((1, d), lambda i, j: (0, 0))
    return pl.pallas_call(
        _wkv_kernel,
        grid=(b, t // tt),
        in_specs=[tile] * 6 + [vec] * 5 + [pl.BlockSpec((gl, gl), lambda i, j: (0, 0))],
        out_specs=tile,
        out_shape=jax.ShapeDtypeStruct((b, t, d), BF16),
        scratch_shapes=[
            pltpu.VMEM((d // gl, gl, gl), F32),
            pltpu.VMEM((n_chunks, 2 * c, d), BF16),
            pltpu.VMEM((n_chunks, c, d), F32),
            pltpu.VMEM((n_chunks, c, d), F32),
            pltpu.VMEM((n_chunks, d, 2 * c), BF16),
            pltpu.VMEM((n_chunks, SUBLANES, d), F32),
        ],
        compiler_params=pltpu.CompilerParams(
            dimension_semantics=("arbitrary", "arbitrary"), vmem_limit_bytes=VMEM_LIMIT_BYTES),
        name="wkv",
    )(r, k, v, lw, a, g, k_k, k_a, r_k, ln_g, ln_b, bd_mask)


CONV_ROWS = 64
CONV_LANES = 128


def _zero_of(v):
    bits = pltpu.bitcast(v[0:SUBLANES, :].astype(F32), jnp.uint32)
    sixteen = jnp.uint32(16)
    bits = lax.shift_right_logical(lax.shift_right_logical(bits, sixteen), sixteen)
    return pltpu.bitcast(bits, F32)[0:1, :]


def _conv_acc(r0, l_lo, l_hi, w_ref, cb_ref, buf_ref, acc_ref, tie=None):
    rb, lb = CONV_ROWS, CONV_LANES
    off = CONV_HALO - (CONV_WIDTH - 1)
    win_rows = rb + CONV_HALO
    bias = cb_ref[...] if tie is None else cb_ref[...] + tie
    zeros = []
    for l0 in range(l_lo, l_hi, lb):
        win = buf_ref[r0:r0 + win_rows, l0:l0 + lb]
        part = jnp.zeros((rb, lb), F32) + bias[:, l0:l0 + lb]
        for s in range(SUBLANES):
            ws = win if s == 0 else pltpu.roll(win, win_rows - s, axis=0)
            for q in range(win_rows // SUBLANES):
                j = SUBLANES * q + s - off
                if 0 <= j < CONV_WIDTH:
                    part = part + ws[SUBLANES * q:SUBLANES * q + rb, :] * w_ref[j:j + 1, l0:l0 + lb]
        acc_ref[r0:r0 + rb, l0:l0 + lb] = part
        zeros.append(_zero_of(part))
    return jnp.concatenate(zeros, axis=1)


def _conv_norm(r0, lng_ref, lnb_ref, out_ref, acc_ref):
    acc = acc_ref[r0:r0 + CONV_ROWS, :]
    mu = jnp.mean(acc, axis=-1, keepdims=True)
    dev = acc - mu
    var = jnp.mean(dev * dev, axis=-1, keepdims=True)
    y = dev * lax.rsqrt(var + LN_EPS) * lng_ref[...] + lnb_ref[...]
    out_ref[r0:r0 + CONV_ROWS, :] = (y * jax.nn.sigmoid(y)).astype(out_ref.dtype)
    return y


def _back_kernel(x_ref, a_ref, u0_ref, un_ref, ga_ref, gc_ref, cw_ref, cb_ref, clg_ref, clb_ref,
                 wa_ref, wc_ref, wo_ref, nff_ref, w1_ref, w2_ref, nfin_ref, o_ref,
                 ubuf_ref, cacc_ref, ccur_ref, cnext_ref):
    d = x_ref.shape[-1]
    dff = w1_ref.shape[1]
    tm = x_ref.shape[0]
    n_conv = tm // CONV_ROWS
    n_mlp = dff // d
    assert n_conv == n_mlp, "one conv row block is paired with each MLP chunk"

    @pl.when(pl.program_id(1) == 0)
    def _():
        ubuf_ref[0:CONV_HALO, :] = jnp.zeros((CONV_HALO, d), F32)
        ubuf_ref[CONV_HALO:, :] = u0_ref[...].astype(F32)
        for i in range(n_conv):
            _conv_acc(i * CONV_ROWS, 0, d, cw_ref, cb_ref, ubuf_ref, cacc_ref)
            _conv_norm(i * CONV_ROWS, clg_ref, clb_ref, ccur_ref, cacc_ref)
        ubuf_ref[0:CONV_HALO, :] = ubuf_ref[tm:tm + CONV_HALO, :]

    ubuf_ref[CONV_HALO:, :] = un_ref[...].astype(F32)
    merged = (ga_ref[...].astype(F32) * _dot(a_ref[...], wa_ref[...])
              + gc_ref[...].astype(F32) * _dot(ccur_ref[...], wc_ref[...]))
    x1 = x_ref[...] + _dot(merged.astype(BF16), wo_ref[...])
    h2 = (x1 * lax.rsqrt(jnp.mean(x1 * x1, axis=-1, keepdims=True) + RMS_EPS) * nff_ref[...]).astype(BF16)
    acc = x1
    conv_zero = None
    for i in range(n_mlp):
        r0 = i * CONV_ROWS
        _conv_acc(r0, 0, d, cw_ref, cb_ref, ubuf_ref, cacc_ref, tie=_zero_of(acc))
        h2i = h2 if conv_zero is None else h2 + conv_zero.astype(BF16)
        hid = jnp.maximum(_dot(h2i, w1_ref[:, i * d:(i + 1) * d]), 0.0)
        acc = acc + _dot((hid * hid).astype(BF16), w2_ref[i * d:(i + 1) * d, :])
        conv_zero = _zero_of(_conv_norm(r0, clg_ref, clb_ref, cnext_ref, cacc_ref))
    acc = acc + conv_zero
    o_ref[...] = acc * lax.rsqrt(jnp.mean(acc * acc, axis=-1, keepdims=True) + RMS_EPS) * nfin_ref[...]
    ubuf_ref[0:CONV_HALO, :] = ubuf_ref[tm:tm + CONV_HALO, :]
    ccur_ref[...] = cnext_ref[...]


def _back(x, a, u, ga, gc, conv_w, conv_b, conv_ln_g, conv_ln_b, wa, wc, wo, nff, w1, w2, nfin, *, tm):
    b, t, d = x.shape
    assert t % tm == 0 and tm % 64 == 0
    nt = t // tm
    tile = pl.BlockSpec((None, tm, d), lambda i, j: (i, j, 0))
    first_tile = pl.BlockSpec((None, tm, d), lambda i, j: (i, 0, 0))
    next_tile = pl.BlockSpec((None, tm, d), lambda i, j: (i, jnp.minimum(j + 1, nt - 1), 0))

    def full(arr, single=False):
        kw = {"pipeline_mode": pl.Buffered(1)} if single else {}
        return pl.BlockSpec(arr.shape, lambda i, j: (0,) * arr.ndim, **kw)

    return pl.pallas_call(
        _back_kernel,
        grid=(b, nt),
        in_specs=[tile, tile, first_tile, next_tile, tile, tile,
                  full(conv_w), full(conv_b), full(conv_ln_g), full(conv_ln_b),
                  full(wa, True), full(wc, True), full(wo, True), full(nff),
                  full(w1, True), full(w2, True), full(nfin)],
        out_specs=tile,
        out_shape=jax.ShapeDtypeStruct((b, t, d), F32),
        scratch_shapes=[
            pltpu.VMEM((tm + CONV_HALO, d), F32),
            pltpu.VMEM((tm, d), F32),
            pltpu.VMEM((tm, d), BF16),
            pltpu.VMEM((tm, d), BF16),
        ],
        compiler_params=pltpu.CompilerParams(
            dimension_semantics=("arbitrary", "arbitrary"), vmem_limit_bytes=VMEM_LIMIT_BYTES),
        name="back",
    )(x, a, u, u, ga, gc, conv_w, conv_b, conv_ln_g, conv_ln_b, wa, wc, wo, nff, w1, w2, nfin)


def _tile(t, want):
    while t % want:
        want //= 2
    return want


def kernel(x, norm_mix_g, w_in, b_gate, mu_rkv, mu_lora, decay_w0, decay_w1, decay_w2, aaa_a0, aaa_a1, aaa_a2, gate_g1, gate_g2, k_k, k_a, r_k, ln_x_g, ln_x_b, w_rwkv_proj, conv_w, conv_b, conv_ln_g, conv_ln_b, w_conv_proj, w_out, norm_ff_g, w_ff1, w_ff2, norm_final_g):
    assert w_in.shape[0] == 1, "single trunk layer"
    t = x.shape[1]
    bf = lambda z: z[0].astype(BF16)
    row = lambda z: z.reshape(1, -1)
    r, k, v, lw, a, g, u, ga, gc = _front(
        x, row(norm_mix_g), bf(w_in), row(b_gate), row(mu_rkv), mu_lora[0],
        row(decay_w0), bf(decay_w1), bf(decay_w2),
        row(aaa_a0), bf(aaa_a1), bf(aaa_a2), bf(gate_g1), bf(gate_g2),
        tm=_tile(t, 512))
    a_out = _wkv(r, k, v, lw, a, g, row(k_k), row(k_a), row(r_k), row(ln_x_g), row(ln_x_b),
                 tt=_tile(t, 512))
    return _back(x, a_out, u, ga, gc, conv_w[0], row(conv_b), row(conv_ln_g), row(conv_ln_b),
                 bf(w_rwkv_proj), bf(w_conv_proj), bf(w_out),
                 row(norm_ff_g), bf(w_ff1), bf(w_ff2), row(norm_final_g), tm=_tile(t, 256))
```

```python
import functools
import math

import jax
import jax.numpy as jnp
from jax import lax
from jax.experimental import pallas as pl
from jax.experimental.pallas import tpu as pltpu

HEAD_SIZE = 64
WKV_CHUNK = 64
HEADS_PER_GROUP = 4
GROUP_LANES = HEADS_PER_GROUP * HEAD_SIZE
WKV_BLOCK_CHUNKS = 4
CONV_WIDTH = 31
CONV_HALO = 32
RMS_EPS = 1e-6
LN_EPS = 1e-5
GN_EPS = 64e-5
DECAY_SCALE = math.exp(-0.5)
SUBLANES = 8
MXU_COLS = 256

VMEM_LIMIT_BYTES = 56 * 1024 * 1024

FRONT_TILE = 512
WKV_TILE = 512
BACK_TILE = 256

F32 = jnp.float32
BF16 = jnp.bfloat16


def _dot(a, b):
    return jnp.dot(a, b, preferred_element_type=F32)


def _dot_nt(a, b):
    return lax.dot_general(a, b, (((1,), (1,)), ((), ())), preferred_element_type=F32)


def _shift_rows(cur, carry_ref):
    rolled = pltpu.roll(cur, 1, axis=0)
    row = lax.broadcasted_iota(jnp.int32, cur.shape, 0)
    prev = jnp.where(row == 0, carry_ref[SUBLANES - 1:SUBLANES, :], rolled)
    carry_ref[...] = cur[cur.shape[0] - SUBLANES:, :]
    return prev


def _front_kernel(x_ref, ng_ref, w_in_ref, bg_ref, mu_rkv_ref, mu_lora_ref,
                  w0_ref, w1_ref, w2_ref, a0_ref, a1_ref, a2_ref, g1_ref, g2_ref,
                  r_ref, k_ref, v_ref, lw_ref, a_ref, g_ref, u_ref, ga_ref, gc_ref,
                  hcar_ref, pcar_ref):
    d = x_ref.shape[-1]

    @pl.when(pl.program_id(1) == 0)
    def _():
        hcar_ref[...] = jnp.zeros_like(hcar_ref)
        pcar_ref[...] = jnp.zeros_like(pcar_ref)

    x = x_ref[...]
    h = x * lax.rsqrt(jnp.mean(x * x, axis=-1, keepdims=True) + RMS_EPS) * ng_ref[...]
    hb = h.astype(BF16)

    dh = _shift_rows(h, hcar_ref) - h
    xw = (h + dh * mu_lora_ref[0:1, :]).astype(BF16)
    xa = (h + dh * mu_lora_ref[1:2, :]).astype(BF16)
    xg = (h + dh * mu_lora_ref[2:3, :]).astype(BF16)
    lora_w = jnp.tanh(_dot(xw, w1_ref[...])).astype(BF16)
    lora_a = _dot(xa, a1_ref[...]).astype(BF16)
    lora_g = jax.nn.sigmoid(_dot(xg, g1_ref[...])).astype(BF16)

    for c, o_ref in enumerate((r_ref, k_ref, v_ref)):
        p = _dot(hb, w_in_ref[:, c * d:(c + 1) * d])
        p_prev = _shift_rows(p, pcar_ref.at[c])
        o_ref[...] = (p + (p_prev - p) * mu_rkv_ref[:, c * d:(c + 1) * d]).astype(o_ref.dtype)

    pa = _dot(hb, w_in_ref[:, 3 * d:4 * d])
    pb = _dot(hb, w_in_ref[:, 4 * d:5 * d])
    u_ref[...] = (pa * jax.nn.sigmoid(pb)).astype(u_ref.dtype)

    for c, o_ref in enumerate((ga_ref, gc_ref)):
        p = _dot(hb, w_in_ref[:, (5 + c) * d:(6 + c) * d]) + bg_ref[:, c * d:(c + 1) * d]
        o_ref[...] = jax.nn.sigmoid(p).astype(o_ref.dtype)

    zw = w0_ref[...] + _dot(lora_w, w2_ref[...])
    lw_ref[...] = -DECAY_SCALE * jax.nn.sigmoid(zw)
    za = a0_ref[...] + _dot(lora_a, a2_ref[...])
    a_ref[...] = jax.nn.sigmoid(za).astype(a_ref.dtype)
    g_ref[...] = _dot(lora_g, g2_ref[...]).astype(g_ref.dtype)


def _front(x, ng, w_in, bg, mu_rkv, mu_lora, w0, w1, w2, a0, a1, a2, g1, g2, *, tm):
    b, t, d = x.shape
    assert t % tm == 0
    tile = pl.BlockSpec((None, tm, d), lambda i, j: (i, j, 0))

    def full(arr, single=False):
        kw = {"pipeline_mode": pl.Buffered(1)} if single else {}
        return pl.BlockSpec(arr.shape, lambda i, j: (0,) * arr.ndim, **kw)

    ins = (x, ng, w_in, bg, mu_rkv, mu_lora, w0, w1, w2, a0, a1, a2, g1, g2)
    in_specs = [tile, full(ng), full(w_in, single=True)] + [full(z) for z in ins[3:]]
    act = jax.ShapeDtypeStruct((b, t, d), BF16)
    out_shape = (act, act, act, jax.ShapeDtypeStruct((b, t, d), F32), act, act, act, act, act)
    return pl.pallas_call(
        _front_kernel,
        grid=(b, t // tm),
        in_specs=in_specs,
        out_specs=[tile] * len(out_shape),
        out_shape=out_shape,
        scratch_shapes=[pltpu.VMEM((SUBLANES, d), F32), pltpu.VMEM((3, SUBLANES, d), F32)],
        compiler_params=pltpu.CompilerParams(
            dimension_semantics=("arbitrary", "arbitrary"), vmem_limit_bytes=VMEM_LIMIT_BYTES),
        name="front",
    )(*ins)


def _cumsum_rows(x):
    n = x.shape[0]
    row = lax.broadcasted_iota(jnp.int32, x.shape, 0) % SUBLANES
    s = 1
    while s < SUBLANES:
        x = x + jnp.where(row >= s, pltpu.roll(x, s, axis=0), 0.0)
        s *= 2
    blocks = [x[0:SUBLANES]]
    for i in range(1, n // SUBLANES):
        blocks.append(x[i * SUBLANES:(i + 1) * SUBLANES] + blocks[-1][SUBLANES - 1:SUBLANES, :])
    return jnp.concatenate(blocks, axis=0)


LANES = 128
HEADS_PER_REG = LANES // HEAD_SIZE


def _bd_rows(blocks):
    lane = lax.broadcasted_iota(jnp.int32, (1, LANES), 1) // HEAD_SIZE
    zero = jnp.zeros_like(blocks[0])
    rows = []
    for hh, blk in enumerate(blocks):
        keep = (lane == hh % HEADS_PER_REG).astype(blk.dtype)
        cols = [zero] * (GROUP_LANES // LANES)
        cols[hh // HEADS_PER_REG] = blk * keep
        rows.append(jnp.concatenate(cols, axis=1))
    return jnp.concatenate(rows, axis=0)


def _bd(x):
    return _bd_rows([x[:, (hh // HEADS_PER_REG) * LANES:(hh // HEADS_PER_REG + 1) * LANES]
                     for hh in range(HEADS_PER_GROUP)])


def _bd_t(x):
    xt = jnp.concatenate([x] * HEADS_PER_REG, axis=0).T.astype(BF16)
    return _bd_rows([xt[hh * HEAD_SIZE:(hh + 1) * HEAD_SIZE, :] for hh in range(HEADS_PER_GROUP)])


def _wkv_kernel(r_ref, k_ref, v_ref, lw_ref, a_ref, g_ref,
                kk_ref, ka_ref, rk_ref, lng_ref, lnb_ref, bdm_ref,
                o_ref, z_ref, qa_ref, oin_ref, uh_ref, bkt_ref, lc_ref):
    c = WKV_CHUNK
    gl = GROUP_LANES
    n_groups = r_ref.shape[1] // gl
    n_chunks = r_ref.shape[0] // c

    @pl.when(pl.program_id(1) == 0)
    def _():
        z_ref[...] = jnp.zeros_like(z_ref)

    row = lax.broadcasted_iota(jnp.int32, (c, gl), 0)
    colj = lax.broadcasted_iota(jnp.int32, (c, gl), 1) % HEAD_SIZE
    strict = row > colj
    incl = row >= colj
    eye = (row == colj).astype(F32)
    bdm = bdm_ref[...]
    brow = lax.broadcasted_iota(jnp.int32, (gl, gl), 0) // HEAD_SIZE
    bcol = lax.broadcasted_iota(jnp.int32, (gl, gl), 1) // HEAD_SIZE
    bd_sel = brow == bcol

    def pass_a(chunk_ids):
        st = []
        chains = []
        for ci in chunk_ids:
            rows = slice(ci * c, (ci + 1) * c)
            for gi in range(n_groups):
                lanes = slice(gi * gl, (gi + 1) * gl)
                k = k_ref[rows, lanes].astype(F32)
                chains.append((ci, rows, lanes, k, k * kk_ref[:, lanes]))
        sq_sums = _dot(jnp.concatenate([(kk * kk).astype(BF16) for *_, kk in chains], axis=0), bdm)
        yield
        for idx, (ci, rows, lanes, k, kk) in enumerate(chains):
            r = r_ref[rows, lanes].astype(F32)
            a_lr = a_ref[rows, lanes].astype(F32)
            lw = lw_ref[rows, lanes]
            kk = kk / jnp.maximum(jnp.sqrt(sq_sums[idx * c:(idx + 1) * c]), 1e-12)
            b = kk * a_lr
            kmod = k * (1.0 + (a_lr - 1.0) * ka_ref[:, lanes])
            cum = _cumsum_rows(lw)
            cum_last = cum[c - 1:c, :]
            e_neg = jnp.exp(-cum)
            e_rem = jnp.exp(cum_last - cum)
            at = (-kk * jnp.exp(cum - lw)).astype(BF16)
            rt = (r * jnp.exp(cum)).astype(BF16)
            bkt_ref[ci, lanes, :] = jnp.concatenate([b * e_rem, kmod * e_rem], axis=0).T.astype(BF16)
            lc_ref[ci, :, lanes] = jnp.broadcast_to(cum_last, (SUBLANES, gl))
            st.append(dict(ci=ci, lanes=lanes, at=at, rt=rt, lhs=jnp.concatenate([at, rt], axis=0),
                           bt=_bd_t(b * e_neg), kt=_bd_t(kmod * e_neg),
                           vbd=_bd(v_ref[rows, lanes])))
        for s in st:
            s["sb"] = _dot(s["lhs"], s["bt"])
        yield
        for s in st:
            s["sk"] = _dot(s["lhs"], s["kt"])
        yield
        for s in st:
            p = jnp.where(strict, s["sb"][:c], 0.0)
            s["tm"] = eye + p
            pb = p.astype(BF16)
            s["p"] = _dot(pb, _bd(pb))
        yield
        for _ in range(4):
            for s in st:
                pb = s["p"].astype(BF16)
                res = _dot(jnp.concatenate([s["tm"].astype(BF16), pb], axis=0), _bd(pb))
                s["tm"] = s["tm"] + res[:c]
                s["p"] = res[c:]
            yield
        for s in st:
            tm = s["tm"] + _dot(s["tm"].astype(BF16), _bd(s["p"].astype(BF16)))
            s["tmb"] = tm.astype(BF16)
        yield
        for s in st:
            ak = jnp.where(strict, s["sk"][:c], 0.0).astype(BF16)
            rk = jnp.where(incl, s["sk"][c:], 0.0).astype(BF16)
            res = _dot(jnp.concatenate([ak, rk], axis=0), s["vbd"])
            s["akv"] = res[:c]
            s["rkv"] = res[c:]
        yield
        for s in st:
            s["a_hat"] = _dot(s["tmb"], _bd(s["at"])).astype(BF16)
        yield
        for s in st:
            s["u_hat"] = _dot(s["tmb"], _bd(s["akv"].astype(BF16)))
        yield
        for s in st:
            s["rb"] = jnp.where(incl, s["sb"][c:], 0.0).astype(BF16)
            s["q_hat"] = s["rt"].astype(F32) + _dot(s["rb"], _bd(s["a_hat"]))
        yield
        for s in st:
            s["o_in"] = _dot(s["rb"], _bd(s["u_hat"].astype(BF16))) + s["rkv"]
        for s in st:
            ci, lanes = s["ci"], s["lanes"]
            qa_ref[ci, 0:c, lanes] = s["q_hat"].astype(BF16)
            qa_ref[ci, c:2 * c, lanes] = s["a_hat"]
            oin_ref[ci, :, lanes] = s["o_in"]
            uh_ref[ci, :, lanes] = s["u_hat"]
        yield

    def pass_b(chunk_ids):
        groups = [slice(gi * gl, (gi + 1) * gl) for gi in range(n_groups)]
        for ci in chunk_ids:
            rows = slice(ci * c, (ci + 1) * c)
            zs = [z_ref[gi] for gi in range(n_groups)]
            res = [_dot(qa_ref[ci, :, lanes], z.astype(BF16)) for lanes, z in zip(groups, zs)]
            yield
            outs = [rs[:c] + oin_ref[ci, :, lanes] for lanes, rs in zip(groups, res)]
            upd = [_dot(bkt_ref[ci, lanes, :],
                        jnp.concatenate([(rs[c:] + uh_ref[ci, :, lanes]).astype(BF16), v_ref[rows, lanes]], axis=0))
                   for lanes, rs in zip(groups, res)]
            for gi, lanes in enumerate(groups):
                w_col = jnp.exp(jnp.broadcast_to(lc_ref[ci, 0:1, lanes], (2 * c, gl)).T)
                z_ref[gi] = jnp.concatenate([w_col, w_col], axis=1) * zs[gi] + jnp.where(bd_sel, upd[gi], 0.0)
            yield
            rkr = []
            for lanes in groups:
                k = k_ref[rows, lanes].astype(F32)
                kmod = k * (1.0 + (a_ref[rows, lanes].astype(F32) - 1.0) * ka_ref[:, lanes])
                rkr.append(r_ref[rows, lanes].astype(F32) * kmod * rk_ref[:, lanes])
            sums = _dot(jnp.concatenate([x.astype(BF16) for x in outs + rkr], axis=0), bdm)
            yield
            devs = [o - sums[gi * c:(gi + 1) * c] * (1.0 / HEAD_SIZE) for gi, o in enumerate(outs)]
            sq_sums = _dot(jnp.concatenate([(dv * dv).astype(BF16) for dv in devs], axis=0), bdm)
            for gi, lanes in enumerate(groups):
                var = sq_sums[gi * c:(gi + 1) * c] * (1.0 / HEAD_SIZE)
                y = devs[gi] * lax.rsqrt(var + GN_EPS) * lng_ref[:, lanes] + lnb_ref[:, lanes]
                y = y + sums[(n_groups + gi) * c:(n_groups + gi + 1) * c] * v_ref[rows, lanes].astype(F32)
                o_ref[rows, lanes] = (y * g_ref[rows, lanes].astype(F32)).astype(o_ref.dtype)
            yield

    def weave(*gens):
        live = list(gens)
        while live:
            for gen in list(live):
                if next(gen, StopIteration) is StopIteration:
                    live.remove(gen)

    cpi = WKV_BLOCK_CHUNKS
    blocks = [range(i, i + cpi) for i in range(0, n_chunks, cpi)]
    weave(pass_a(blocks[0]))
    for prev, cur in zip(blocks[:-1], blocks[1:]):
        weave(pass_a(cur), pass_b(prev))
    weave(pass_b(blocks[-1]))


def _wkv(r, k, v, lw, a, g, k_k, k_a, r_k, ln_g, ln_b, *, tt):
    b, t, d = r.shape
    gl = GROUP_LANES
    c = WKV_CHUNK
    assert t % tt == 0 and tt % c == 0 and d % gl == 0
    n_chunks = tt // c
    assert n_chunks % WKV_BLOCK_CHUNKS == 0
    head_of = jnp.arange(gl) // HEAD_SIZE
    bd_mask = (head_of[:, None] == head_of[None, :]).astype(BF16)
    tile = pl.BlockSpec((None, tt, d), lambda i, j: (i, j, 0))
    vec = pl.BlockSpec((1, d), lambda i, j: (0, 0))
    return pl.pallas_call(
        _wkv_kernel,
        grid=(b, t // tt),
        in_specs=[tile] * 6 + [vec] * 5 + [pl.BlockSpec((gl, gl), lambda i, j: (0, 0))],
        out_specs=tile,
        out_shape=jax.ShapeDtypeStruct((b, t, d), BF16),
        scratch_shapes=[
            pltpu.VMEM((d // gl, gl, gl), F32),
            pltpu.VMEM((n_chunks, 2 * c, d), BF16),
            pltpu.VMEM((n_chunks, c, d), F32),
            pltpu.VMEM((n_chunks, c, d), F32),
            pltpu.VMEM((n_chunks, d, 2 * c), BF16),
            pltpu.VMEM((n_chunks, SUBLANES, d), F32),
        ],
        compiler_params=pltpu.CompilerParams(
            dimension_semantics=("arbitrary", "arbitrary"), vmem_limit_bytes=VMEM_LIMIT_BYTES),
        name="wkv",
    )(r, k, v, lw, a, g, k_k, k_a, r_k, ln_g, ln_b, bd_mask)


CONV_ROWS = 64
CONV_LANES = 128


def _zero_of(v):
    bits = pltpu.bitcast(v[0:SUBLANES, :].astype(F32), jnp.uint32)
    sixteen = jnp.uint32(16)
    bits = lax.shift_right_logical(lax.shift_right_logical(bits, sixteen), sixteen)
    return pltpu.bitcast(bits, F32)[0:1, :]


def _conv_acc(r0, l_lo, l_hi, w_ref, cb_ref, buf_ref, acc_ref, tie=None):
    rb, lb = CONV_ROWS, CONV_LANES
    off = CONV_HALO - (CONV_WIDTH - 1)
    win_rows = rb + CONV_HALO
    bias = cb_ref[...] if tie is None else cb_ref[...] + tie
    zeros = []
    for l0 in range(l_lo, l_hi, lb):
        win = buf_ref[r0:r0 + win_rows, l0:l0 + lb]
        part = jnp.zeros((rb, lb), F32) + bias[:, l0:l0 + lb]
        for s in range(SUBLANES):
            ws = win if s == 0 else pltpu.roll(win, win_rows - s, axis=0)
            for q in range(win_rows // SUBLANES):
                j = SUBLANES * q + s - off
                if 0 <= j < CONV_WIDTH:
                    part = part + ws[SUBLANES * q:SUBLANES * q + rb, :] * w_ref[j:j + 1, l0:l0 + lb]
        acc_ref[r0:r0 + rb, l0:l0 + lb] = part
        zeros.append(_zero_of(part))
    return jnp.concatenate(zeros, axis=1)


def _conv_norm(r0, lng_ref, lnb_ref, out_ref, acc_ref):
    acc = acc_ref[r0:r0 + CONV_ROWS, :]
    mu = jnp.mean(acc, axis=-1, keepdims=True)
    dev = acc - mu
    var = jnp.mean(dev * dev, axis=-1, keepdims=True)
    y = dev * lax.rsqrt(var + LN_EPS) * lng_ref[...] + lnb_ref[...]
    out_ref[r0:r0 + CONV_ROWS, :] = (y * jax.nn.sigmoid(y)).astype(out_ref.dtype)
    return y


def _back_kernel(x_ref, a_ref, u0_ref, un_ref, ga_ref, gc_ref, cw_ref, cb_ref, clg_ref, clb_ref,
                 wa_ref, wc_ref, wo_ref, nff_ref, w1_ref, w2_ref, nfin_ref, o_ref,
                 ubuf_ref, cacc_ref, ccur_ref, cnext_ref):
    d = x_ref.shape[-1]
    dff = w1_ref.shape[1]
    tm = x_ref.shape[0]
    n_conv = tm // CONV_ROWS
    n_mlp = dff // d
    assert n_conv == n_mlp, "one conv row block is paired with each MLP chunk"

    @pl.when(pl.program_id(1) == 0)
    def _():
        ubuf_ref[0:CONV_HALO, :] = jnp.zeros((CONV_HALO, d), F32)
        ubuf_ref[CONV_HALO:, :] = u0_ref[...].astype(F32)
        for i in range(n_conv):
            _conv_acc(i * CONV_ROWS, 0, d, cw_ref, cb_ref, ubuf_ref, cacc_ref)
            _conv_norm(i * CONV_ROWS, clg_ref, clb_ref, ccur_ref, cacc_ref)
        ubuf_ref[0:CONV_HALO, :] = ubuf_ref[tm:tm + CONV_HALO, :]

    ubuf_ref[CONV_HALO:, :] = un_ref[...].astype(F32)
    merged = (ga_ref[...].astype(F32) * _dot(a_ref[...], wa_ref[...])
              + gc_ref[...].astype(F32) * _dot(ccur_ref[...], wc_ref[...]))
    x1 = x_ref[...] + _dot(merged.astype(BF16), wo_ref[...])
    h2 = (x1 * lax.rsqrt(jnp.mean(x1 * x1, axis=-1, keepdims=True) + RMS_EPS) * nff_ref[...]).astype(BF16)
    acc = x1
    conv_zero = None
    for i in range(n_mlp):
        r0 = i * CONV_ROWS
        _conv_acc(r0, 0, d, cw_ref, cb_ref, ubuf_ref, cacc_ref, tie=_zero_of(acc))
        h2i = h2 if conv_zero is None else h2 + conv_zero.astype(BF16)
        hid = jnp.maximum(_dot(h2i, w1_ref[:, i * d:(i + 1) * d]), 0.0)
        acc = acc + _dot((hid * hid).astype(BF16), w2_ref[i * d:(i + 1) * d, :])
        conv_zero = _zero_of(_conv_norm(r0, clg_ref, clb_ref, cnext_ref, cacc_ref))
    acc = acc + conv_zero
    o_ref[...] = acc * lax.rsqrt(jnp.mean(acc * acc, axis=-1, keepdims=True) + RMS_EPS) * nfin_ref[...]
    ubuf_ref[0:CONV_HALO, :] = ubuf_ref[tm:tm + CONV_HALO, :]
    ccur_ref[...] = cnext_ref[...]


def _back(x, a, u, ga, gc, conv_w, conv_b, conv_ln_g, conv_ln_b, wa, wc, wo, nff, w1, w2, nfin, *, tm):
    b, t, d = x.shape
    assert t % tm == 0 and tm % 64 == 0
    nt = t // tm
    tile = pl.BlockSpec((None, tm, d), lambda i, j: (i, j, 0))
    first_tile = pl.BlockSpec((None, tm, d), lambda i, j: (i, 0, 0))
    next_tile = pl.BlockSpec((None, tm, d), lambda i, j: (i, jnp.minimum(j + 1, nt - 1), 0))

    def full(arr, single=False):
        kw = {"pipeline_mode": pl.Buffered(1)} if single else {}
        return pl.BlockSpec(arr.shape, lambda i, j: (0,) * arr.ndim, **kw)

    return pl.pallas_call(
        _back_kernel,
        grid=(b, nt),
        in_specs=[tile, tile, first_tile, next_tile, tile, tile,
                  full(conv_w), full(conv_b), full(conv_ln_g), full(conv_ln_b),
                  full(wa, True), full(wc, True), full(wo, True), full(nff),
                  full(w1, True), full(w2, True), full(nfin)],
        out_specs=tile,
        out_shape=jax.ShapeDtypeStruct((b, t, d), F32),
        scratch_shapes=[
            pltpu.VMEM((tm + CONV_HALO, d), F32),
            pltpu.VMEM((tm, d), F32),
            pltpu.VMEM((tm, d), BF16),
            pltpu.VMEM((tm, d), BF16),
        ],
        compiler_params=pltpu.CompilerParams(
            dimension_semantics=("arbitrary", "arbitrary"), vmem_limit_bytes=VMEM_LIMIT_BYTES),
        name="back",
    )(x, a, u, u, ga, gc, conv_w, conv_b, conv_ln_g, conv_ln_b, wa, wc, wo, nff, w1, w2, nfin)


def _tile(t, want):
    while t % want:
        want //= 2
    return want


def kernel(x, norm_mix_g, w_in, b_gate, mu_rkv, mu_lora, decay_w0, decay_w1, decay_w2, aaa_a0, aaa_a1, aaa_a2, gate_g1, gate_g2, k_k, k_a, r_k, ln_x_g, ln_x_b, w_rwkv_proj, conv_w, conv_b, conv_ln_g, conv_ln_b, w_conv_proj, w_out, norm_ff_g, w_ff1, w_ff2, norm_final_g):
    assert w_in.shape[0] == 1, "single trunk layer"
    t = x.shape[1]
    bf = lambda z: z[0].astype(BF16)
    row = lambda z: z.reshape(1, -1)
    r, k, v, lw, a, g, u, ga, gc = _front(
        x, row(norm_mix_g), bf(w_in), row(b_gate), row(mu_rkv), mu_lora[0],
        row(decay_w0), bf(decay_w1), bf(decay_w2),
        row(aaa_a0), bf(aaa_a1), bf(aaa_a2), bf(gate_g1), bf(gate_g2),
        tm=_tile(t, FRONT_TILE))
    a_out = _wkv(r, k, v, lw, a, g, row(k_k), row(k_a), row(r_k), row(ln_x_g), row(ln_x_b),
                 tt=_tile(t, WKV_TILE))
    return _back(x, a_out, u, ga, gc, conv_w[0], row(conv_b), row(conv_ln_g), row(conv_ln_b),
                 bf(w_rwkv_proj), bf(w_conv_proj), bf(w_out),
                 row(norm_ff_g), bf(w_ff1), bf(w_ff2), row(norm_final_g), tm=_tile(t, BACK_TILE))
```
